```python
import math
import jax, jax.numpy as jnp
from jax import lax
import numpy as np

D_MODEL = 1024
BATCH = 16
SEQ = 4096
DEPTH = 1
DEC_BATCH = 128
DEC_SEQ = 8
PAST_LEN = 8192
PAGE_SIZE = 128

ATTN_WIDTH = D_MODEL // 2
CONV_CH = D_MODEL - ATTN_WIDTH
HEAD_D = 64
N_HEADS = ATTN_WIDTH // (2 * HEAD_D)
CONV_W = 31
D_FF = ((8 * D_MODEL // 3 + 255) // 256) * 256
PLE_DIM = 256
N_BUCKETS = 32
MAX_DIST = 128
Q_BLOCK = 128
EPS = 1e-6
NEG = -1e30
IN_COLS = 3 * ATTN_WIDTH + 2 * CONV_CH

kernel_name = "hymba_diffattn_conformer_conv_decode_step"


def rmsnorm(x, g):
    xf = x.astype(jnp.float32)
    y = xf * lax.rsqrt(jnp.mean(xf * xf, axis=-1, keepdims=True) + EPS)
    return (y * g.astype(jnp.float32)).astype(x.dtype)


def layernorm(x, g, b):
    xf = x.astype(jnp.float32)
    mu = jnp.mean(xf, axis=-1, keepdims=True)
    xc = xf - mu
    y = xc * lax.rsqrt(jnp.mean(xc * xc, axis=-1, keepdims=True) + EPS)
    return (y * g.astype(jnp.float32) + b.astype(jnp.float32)).astype(x.dtype)


def swiglu_half(h, norm_g, wg, wu, wd):
    u = rmsnorm(h, norm_g)
    return h + 0.5 * ((jax.nn.silu(u @ wg) * (u @ wu)) @ wd)


def rel_bucket(rel):
    n = jnp.maximum(rel, 0)
    max_exact = N_BUCKETS // 2
    nf = jnp.maximum(n, max_exact).astype(jnp.float32)
    large = max_exact + (jnp.log(nf / max_exact) / math.log(MAX_DIST / max_exact)
                         * (N_BUCKETS - max_exact)).astype(jnp.int32)
    large = jnp.minimum(large, N_BUCKETS - 1)
    return jnp.where(n < max_exact, n, large)


def diff_attend(q, k, v, q_pos, k_pos, rel_bias, lam):
    s = jnp.einsum('bqhmd,bkhmd->bmhqk', q, k).astype(jnp.float32) * (HEAD_D ** -0.5)
    bias = rel_bias[rel_bucket(q_pos[:, None] - k_pos[None, :])]
    s = s + jnp.transpose(bias, (2, 0, 1)).astype(jnp.float32)[None, None]
    s = jnp.where((k_pos[None, :] <= q_pos[:, None])[None, None, None], s, NEG)
    a = jax.nn.softmax(s, axis=-1)
    a = a[:, 0] - lam * a[:, 1]
    return jnp.einsum('bhqk,bkhe->bqhe', a.astype(v.dtype), v)


def in_projection(h, norm_g, w_in):
    z = rmsnorm(h, norm_g) @ w_in
    lead = h.shape[:2]
    A = ATTN_WIDTH
    q = z[..., :A].reshape(*lead, N_HEADS, 2, HEAD_D)
    k = z[..., A:2 * A].reshape(*lead, N_HEADS, 2, HEAD_D)
    v = z[..., 2 * A:3 * A].reshape(*lead, N_HEADS, 2 * HEAD_D)
    g = z[..., 3 * A:]
    u = g[..., :CONV_CH] * jax.nn.sigmoid(g[..., CONV_CH:])
    return q, k, v, u


def diff_lambda(lq1, lk1, lq2, lk2, lam_init):
    f = lambda a, b: jnp.exp(jnp.sum(a.astype(jnp.float32) * b.astype(jnp.float32)))
    return f(lq1, lk1) - f(lq2, lk2) + lam_init


def attn_finish(o, subln_g, lam_init):
    o = rmsnorm(o, subln_g) * (1.0 - lam_init)
    return o.reshape(*o.shape[:2], ATTN_WIDTH)


def depthwise_valid(u_ext, w, b):
    y = lax.conv_general_dilated(u_ext, w[:, None, :].astype(u_ext.dtype), window_strides=(1,),
                                 padding='VALID', dimension_numbers=('NWC', 'WIO', 'NWC'),
                                 feature_group_count=CONV_CH)
    return y + b


def conv_finish(c, ln_g, ln_b):
    return jax.nn.silu(layernorm(c, ln_g, ln_b))


def ple_add(h, p, norm_g, w_gate, w_proj):
    return h + (p @ w_proj) * jax.nn.sigmoid(rmsnorm(h, norm_g) @ w_gate)


def setup_inputs(seed: int = 0) -> dict:
    key = jax.random.key(seed)
    ks = iter(jax.random.split(key, 48))
    nrm = lambda shape, scale: jax.random.normal(next(ks), shape, jnp.float32) * scale
    gain = lambda shape: 1.0 + nrm(shape, 0.01)
    n_pages = PAST_LEN // PAGE_SIZE
    n_used = DEC_BATCH * n_pages
    n_pool = n_used + max(1, n_used // 4)
    L, D, F = DEPTH, D_MODEL, D_FF
    return {
        "x_prompt": nrm((BATCH, SEQ, D), 1.0),
        "x_sample": nrm((DEC_BATCH, DEC_SEQ, D), 1.0),
        "cache_k": nrm((L, n_pool, PAGE_SIZE, N_HEADS, 2 * HEAD_D), 1.0),
        "cache_v": nrm((L, n_pool, PAGE_SIZE, N_HEADS, 2 * HEAD_D), 1.0),
        "state_conv": nrm((L, DEC_BATCH, CONV_W - 1, CONV_CH), 0.5),
        "page_table": jax.random.permutation(next(ks), n_pool)[:n_used].reshape(DEC_BATCH, n_pages).astype(jnp.int32),
        "p_prompt": nrm((L, BATCH, SEQ, PLE_DIM), 1.0),
        "p_sample": nrm((L, DEC_BATCH, DEC_SEQ, PLE_DIM), 1.0),
        "rel_bias": nrm((N_BUCKETS, N_HEADS), 0.5),
        "ffn1_norm": gain((L, D)),
        "ffn1_w_gate": nrm((L, D, F), D ** -0.5),
        "ffn1_w_up": nrm((L, D, F), D ** -0.5),
        "ffn1_w_down": nrm((L, F, D), F ** -0.5),
        "mix_norm": gain((L, D)),
        "w_in": nrm((L, D, IN_COLS), D ** -0.5),
        "lambda_q1": nrm((L, HEAD_D), 0.1),
        "lambda_k1": nrm((L, HEAD_D), 0.1),
        "lambda_q2": nrm((L, HEAD_D), 0.1),
        "lambda_k2": nrm((L, HEAD_D), 0.1),
        "attn_subln": gain((L, 2 * HEAD_D)),
        "conv_w": nrm((L, CONV_W, CONV_CH), CONV_W ** -0.5),
        "conv_b": nrm((L, CONV_CH), 0.01),
        "conv_ln_g": gain((L, CONV_CH)),
        "conv_ln_b": nrm((L, CONV_CH), 0.01),
        "w_out": nrm((L, ATTN_WIDTH + CONV_CH, D), (ATTN_WIDTH + CONV_CH) ** -0.5),
        "ffn2_norm": gain((L, D)),
        "ffn2_w_gate": nrm((L, D, F), D ** -0.5),
        "ffn2_w_up": nrm((L, D, F), D ** -0.5),
        "ffn2_w_down": nrm((L, F, D), F ** -0.5),
        "ple_norm": gain((L, D)),
        "w_ple_gate": nrm((L, D, D), D ** -0.5),
        "w_ple_proj": nrm((L, PLE_DIM, D), PLE_DIM ** -0.5),
        "final_norm": gain((D,)),
    }


def reference(x_prompt, x_sample, cache_k, cache_v, state_conv, page_table, p_prompt, p_sample,
              rel_bias, ffn1_norm, ffn1_w_gate, ffn1_w_up, ffn1_w_down, mix_norm, w_in,
              lambda_q1, lambda_k1, lambda_q2, lambda_k2, attn_subln, conv_w, conv_b,
              conv_ln_g, conv_ln_b, w_out, ffn2_norm, ffn2_w_gate, ffn2_w_up, ffn2_w_down,
              ple_norm, w_ple_gate, w_ple_proj, final_norm):
    hp, hs = x_prompt, x_sample
    b, seq = hp.shape[0], hp.shape[1]
    db, dec_seq = hs.shape[0], hs.shape[1]
    past = page_table.shape[1] * cache_k.shape[2]
    q_blk = min(Q_BLOCK, seq)
    nb = seq // q_blk
    pos_p = jnp.arange(seq, dtype=jnp.int32)
    pos_sq = past + jnp.arange(dec_seq, dtype=jnp.int32)
    pos_sk = jnp.arange(past + dec_seq, dtype=jnp.int32)
    kp_l, vp_l, cp_l, ks_l, vs_l, cs_l = [], [], [], [], [], []
    for i in range(DEPTH):
        lam_init = 0.8 - 0.6 * math.exp(-0.3 * i)
        lam = diff_lambda(lambda_q1[i], lambda_k1[i], lambda_q2[i], lambda_k2[i], lam_init)
        hp = swiglu_half(hp, ffn1_norm[i], ffn1_w_gate[i], ffn1_w_up[i], ffn1_w_down[i])
        hs = swiglu_half(hs, ffn1_norm[i], ffn1_w_gate[i], ffn1_w_up[i], ffn1_w_down[i])

        q, k, v, u = in_projection(hp, mix_norm[i], w_in[i])
        qb = q.reshape(b, nb, q_blk, N_HEADS, 2, HEAD_D).swapaxes(0, 1)
        pb = pos_p.reshape(nb, q_blk)
        o = lax.map(lambda a: diff_attend(a[0], k, v, a[1], pos_p, rel_bias, lam), (qb, pb))
        o = o.swapaxes(0, 1).reshape(b, seq, N_HEADS, 2 * HEAD_D)
        u_ext = jnp.pad(u, ((0, 0), (CONV_W - 1, 0), (0, 0)))
        c = conv_finish(depthwise_valid(u_ext, conv_w[i], conv_b[i]), conv_ln_g[i], conv_ln_b[i])
        hp = hp + jnp.concatenate([attn_finish(o, attn_subln[i], lam_init), c], axis=-1) @ w_out[i]
        kp_l.append(k.reshape(b, seq, N_HEADS, 2 * HEAD_D))
        vp_l.append(v)
        cp_l.append(u[:, seq - (CONV_W - 1):])

        q, k, v, u = in_projection(hs, mix_norm[i], w_in[i])
        k_past = cache_k[i][page_table].reshape(db, past, N_HEADS, 2, HEAD_D).astype(k.dtype)
        v_past = cache_v[i][page_table].reshape(db, past, N_HEADS, 2 * HEAD_D).astype(v.dtype)
        k_all = jnp.concatenate([k_past, k], axis=1)
        v_all = jnp.concatenate([v_past, v], axis=1)
        o = diff_attend(q, k_all, v_all, pos_sq, pos_sk, rel_bias, lam)
        u_ext = jnp.concatenate([state_conv[i].astype(u.dtype), u], axis=1)
        c = conv_finish(depthwise_valid(u_ext, conv_w[i], conv_b[i]), conv_ln_g[i], conv_ln_b[i])
        hs = hs + jnp.concatenate([attn_finish(o, attn_subln[i], lam_init), c], axis=-1) @ w_out[i]
        ks_l.append(k.reshape(db, dec_seq, N_HEADS, 2 * HEAD_D))
        vs_l.append(v)
        cs_l.append(u_ext[:, u_ext.shape[1] - (CONV_W - 1):])

        hp = swiglu_half(hp, ffn2_norm[i], ffn2_w_gate[i], ffn2_w_up[i], ffn2_w_down[i])
        hs = swiglu_half(hs, ffn2_norm[i], ffn2_w_gate[i], ffn2_w_up[i], ffn2_w_down[i])
        hp = ple_add(hp, p_prompt[i], ple_norm[i], w_ple_gate[i], w_ple_proj[i])
        hs = ple_add(hs, p_sample[i], ple_norm[i], w_ple_gate[i], w_ple_proj[i])

    y_prompt = rmsnorm(hp, final_norm)
    y_sample = rmsnorm(hs, final_norm)
    return (y_prompt, y_sample, jnp.stack(kp_l), jnp.stack(vp_l), jnp.stack(cp_l),
            jnp.stack(ks_l), jnp.stack(vs_l), jnp.stack(cs_l))
```

```python
import functools
import math

import jax
import jax.numpy as jnp
from jax import lax
from jax.experimental import pallas as pl
from jax.experimental.pallas import tpu as pltpu

N_HEADS = 4
HEAD_D = 64
HEAD_W = 2 * HEAD_D
ATTN_W = N_HEADS * HEAD_W
CONV_W = 31
N_BUCKETS = 32
MAX_EXACT = N_BUCKETS // 2
MAX_DIST = 128
EPS = 1e-6
NEG = -1e30
LAM_INIT = 0.8 - 0.6 * math.exp(-0.3 * 0)

LANES = 128
SUBLANES = 8
VMEM_LIMIT_BYTES = 56 * 1024 * 1024

FF_CHUNK = 256
ROW_TILE = 512
ATT_TQ = 256
ATT_TK = 256
DEC_PAGES_PER_STEP = 16
BF16 = jnp.bfloat16
F32 = jnp.float32


def _rms(x, g):
    return x * lax.rsqrt(jnp.mean(x * x, axis=-1, keepdims=True) + EPS) * g


def _const_spec(shape):
    nd = len(shape)
    return pl.BlockSpec(shape, lambda *_: (0,) * nd, pipeline_mode=pl.Buffered(1))


def _swiglu_half(x, g_ref, wgu_ref, wd_ref, acc_ref):
    u = _rms(x, g_ref[...]).astype(BF16)
    n_chunks = wgu_ref.shape[0]
    for c in range(n_chunks):
        gu = jnp.dot(u, wgu_ref[c], preferred_element_type=F32)
        g = gu[:, :FF_CHUNK]
        a = (g * jax.nn.sigmoid(g) * gu[:, FF_CHUNK:]).astype(BF16)
        d = jnp.dot(a, wd_ref[c], preferred_element_type=F32)
        if c == 0:
            acc_ref[...] = d
        else:
            acc_ref[...] += d
    return x + 0.5 * acc_ref[...]


def _ffn_inproj_kernel(x_ref, g1_ref, wgu_ref, wd_ref, gm_ref, win_ref,
                       h_ref, q_ref, k_ref, v_ref, kb_ref, vb_ref, u_ref, acc_ref):
    h = _swiglu_half(x_ref[...], g1_ref, wgu_ref, wd_ref, acc_ref)
    h_ref[...] = h
    un = _rms(h, gm_ref[...]).astype(BF16)
    a = ATTN_W
    q = jnp.dot(un, win_ref[:, 0:a], preferred_element_type=F32)
    q_ref[...] = (q * (HEAD_D ** -0.5)).astype(BF16)
    k = jnp.dot(un, win_ref[:, a:2 * a], preferred_element_type=F32)
    k_ref[...] = k
    kb_ref[...] = k.astype(BF16)
    v = jnp.dot(un, win_ref[:, 2 * a:3 * a], preferred_element_type=F32)
    v_ref[...] = v
    vb_ref[...] = v.astype(BF16)
    ga = jnp.dot(un, win_ref[:, 3 * a:4 * a], preferred_element_type=F32)
    gb = jnp.dot(un, win_ref[:, 4 * a:5 * a], preferred_element_type=F32)
    u_ref[...] = ga * jax.nn.sigmoid(gb)


def _ffn_inproj(x, g1, wgu, wd, gm, win):
    t, d = x.shape
    tm = min(ROW_TILE, t)
    assert t % tm == 0
    row = lambda w: pl.BlockSpec((tm, w), lambda i: (i, 0))
    outs = [
        jax.ShapeDtypeStruct((t, d), F32),
        jax.ShapeDtypeStruct((t, ATTN_W), BF16),
        jax.ShapeDtypeStruct((t, ATTN_W), F32),
        jax.ShapeDtypeStruct((t, ATTN_W), F32),
        jax.ShapeDtypeStruct((t, ATTN_W), BF16),
        jax.ShapeDtypeStruct((t, ATTN_W), BF16),
        jax.ShapeDtypeStruct((t, ATTN_W), F32),
    ]
    return pl.pallas_call(
        _ffn_inproj_kernel,
        out_shape=outs,
        grid=(t // tm,),
        in_specs=[row(d), _const_spec(g1.shape), _const_spec(wgu.shape), _const_spec(wd.shape),
                  _const_spec(gm.shape), _const_spec(win.shape)],
        out_specs=[row(d)] + [row(ATTN_W)] * 6,
        scratch_shapes=[pltpu.VMEM((tm, d), F32)],
        compiler_params=pltpu.CompilerParams(
            dimension_semantics=("arbitrary",), vmem_limit_bytes=VMEM_LIMIT_BYTES),
        name="ffn1_inproj",
    )(x, g1, wgu, wd, gm, win)


def _bias_of_distance(dist, rb_ref, h):
    n = jnp.maximum(dist, 0)
    nf = jnp.maximum(n, MAX_EXACT).astype(F32)
    large = MAX_EXACT + (jnp.log(nf / MAX_EXACT) / math.log(MAX_DIST / MAX_EXACT)
                         * (N_BUCKETS - MAX_EXACT)).astype(jnp.int32)
    large = jnp.minimum(large, N_BUCKETS - 1)
    bucket = jnp.where(n < MAX_EXACT, n, large)
    far = rb_ref[N_BUCKETS - 1, h]
    val = jnp.zeros(dist.shape, F32)
    for i in range(N_BUCKETS - 1):
        val = jnp.where(bucket == i, rb_ref[i, h] - far, val)
    return jnp.where(dist >= 0, val, NEG)


def _bias_kernel(rb_ref, lq1_ref, lk1_ref, lq2_ref, lk2_ref, bp_ref, bd_ref, lam_ref, *, past):
    tq, tk = bp_ref.shape[2], bp_ref.shape[3]
    r = lax.broadcasted_iota(jnp.int32, (tq, tk), 0)
    c = lax.broadcasted_iota(jnp.int32, (tq, tk), 1)
    n_new = bd_ref.shape[1] // 2
    page = bd_ref.shape[2] // 2
    qi = lax.broadcasted_iota(jnp.int32, (2 * n_new, page), 0) % n_new
    cj = lax.broadcasted_iota(jnp.int32, (2 * n_new, page), 1)
    for h in range(N_HEADS):
        bp_ref[h, 0] = _bias_of_distance(tk + r - c, rb_ref, h)
        bp_ref[h, 1] = _bias_of_distance(r - c, rb_ref, h)
        bd_ref[h, :, 0:page] = _bias_of_distance(qi + page - cj, rb_ref, h)
        new = _bias_of_distance(qi - cj, rb_ref, h)
        bd_ref[h, :, page:2 * page] = jnp.where(cj < n_new, new, NEG)
    e1 = jnp.exp(jnp.sum(lq1_ref[...] * lk1_ref[...], axis=-1, keepdims=True))
    e2 = jnp.exp(jnp.sum(lq2_ref[...] * lk2_ref[...], axis=-1, keepdims=True))
    lam_ref[...] = jnp.broadcast_to(e1 - e2 + LAM_INIT, lam_ref.shape)


def _bias_tables(rel_bias, lq1, lk1, lq2, lk2, tq, tk, n_new, page, past):
    vm = lambda shape: pl.BlockSpec(shape, lambda: (0,) * len(shape))
    return pl.pallas_call(
        functools.partial(_bias_kernel, past=past),
        out_shape=[jax.ShapeDtypeStruct((N_HEADS, 2, tq, tk), F32),
                   jax.ShapeDtypeStruct((N_HEADS, 2 * n_new, 2 * page), F32),
                   jax.ShapeDtypeStruct((SUBLANES, LANES), F32)],
        in_specs=[pl.BlockSpec(memory_space=pltpu.SMEM)] + [vm(lq1.shape)] * 4,
        out_specs=[vm((N_HEADS, 2, tq, tk)), vm((N_HEADS, 2 * n_new, 2 * page)), vm((SUBLANES, LANES))],
        name="bias_tables",
    )(rel_bias, lq1, lk1, lq2, lk2)


def _stack_maps(qh):
    lane = lax.broadcasted_iota(jnp.int32, qh.shape, 1)
    zero = jnp.zeros_like(qh)
    return jnp.concatenate([jnp.where(lane < HEAD_D, qh, zero),
                            jnp.where(lane < HEAD_D, zero, qh)], axis=0)


def _softmax_step(s, m, l, acc, v):
    m_new = jnp.maximum(m, jnp.max(s, axis=-1, keepdims=True))
    p = jnp.exp(s - m_new)
    alpha = jnp.exp(m - m_new)
    l = alpha * l + jnp.sum(p, axis=-1, keepdims=True)
    acc = alpha * acc + jnp.dot(p.astype(BF16), v, preferred_element_type=F32)
    return m_new, l, acc


def _diff_finish(l, acc, lam, g):
    n = acc.shape[0] // 2
    o = acc[:n] / l[:n] - lam * (acc[n:] / l[n:])
    return _rms(o, g) * (1.0 - LAM_INIT)


def _prompt_attn_kernel(q_ref, k_ref, v_ref, bias_ref, lam_ref, g_ref, o_ref):
    i = pl.program_id(1)
    tq = q_ref.shape[0]
    tk = bias_ref.shape[3]
    assert tq == tk
    lam = lam_ref[0:1, :]
    g = g_ref[...]
    for h in range(N_HEADS):
        cols = slice(h * HEAD_W, (h + 1) * HEAD_W)
        qs = _stack_maps(q_ref[:, cols])

        def scores(j):
            start = pl.multiple_of(j * tk, tk)
            kt = k_ref[pl.ds(start, tk), cols]
            vt = v_ref[pl.ds(start, tk), cols]
            s = lax.dot_general(qs, kt, (((1,), (1,)), ((), ())), preferred_element_type=F32)
            return s, vt

        def far_step(j, carry):
            s, vt = scores(j)
            return _softmax_step(s, *carry, vt)

        def near_step(j, carry):
            s, vt = scores(j)
            b = bias_ref[h, j - i + 1]
            s = s + jnp.concatenate([b, b], axis=0)
            return _softmax_step(s, *carry, vt)

        init = (jnp.full((2 * tq, 1), NEG, F32), jnp.zeros((2 * tq, 1), F32),
                jnp.zeros((2 * tq, HEAD_W), F32))
        first_near = jnp.maximum(i - 1, 0)
        carry = lax.fori_loop(0, first_near, far_step, init)
        _, l, acc = lax.fori_loop(first_near, i + 1, near_step, carry)
        o_ref[:, cols] = _diff_finish(l, acc, lam, g).astype(o_ref.dtype)


def _prompt_attention(q, kb, vb, bias_p, lam, g, batch, seq):
    tq = bias_p.shape[2]
    nq = seq // tq
    return pl.pallas_call(
        _prompt_attn_kernel,
        out_shape=jax.ShapeDtypeStruct((batch * seq, ATTN_W), BF16),
        grid=(batch, nq),
        in_specs=[pl.BlockSpec((tq, ATTN_W), lambda b, i: (b * nq + i, 0)),
                  pl.BlockSpec((seq, ATTN_W), lambda b, i: (b, 0)),
                  pl.BlockSpec((seq, ATTN_W), lambda b, i: (b, 0)),
                  _const_spec(bias_p.shape), _const_spec(lam.shape), _const_spec(g.shape)],
        out_specs=pl.BlockSpec((tq, ATTN_W), lambda b, i: (b * nq + i, 0)),
        compiler_params=pltpu.CompilerParams(
            dimension_semantics=("arbitrary", "arbitrary"), vmem_limit_bytes=VMEM_LIMIT_BYTES),
        name="prompt_attention",
    )(q, kb, vb, bias_p, lam, g)


def _decode_attn_kernel(pt_ref, *refs, pages_per_step, page):
    del pt_ref
    pps = pages_per_step
    kp = refs[0:pps]
    vp = refs[pps:2 * pps]
    q_ref, kn_ref, vn_ref, bias_ref, lam_ref, g_ref, o_ref, m_ref, l_ref, acc_ref = refs[2 * pps:]
    c = pl.program_id(1)
    last = pl.num_programs(1) - 1
    n_new = q_ref.shape[1]

    @pl.when(c == 0)
    def _():
        m_ref[...] = jnp.full(m_ref.shape, NEG, F32)
        l_ref[...] = jnp.zeros(l_ref.shape, F32)
        acc_ref[...] = jnp.zeros(acc_ref.shape, F32)

    is_last = (c == last).astype(F32)
    for h in range(N_HEADS):
        cols = slice(h * HEAD_W, (h + 1) * HEAD_W)
        qs = _stack_maps(q_ref[0, :, cols])
        kt = jnp.concatenate(
            [kp[j][pl.ds(h, page, stride=N_HEADS), :].astype(BF16) for j in range(pps)], axis=0)
        vt = jnp.concatenate(
            [vp[j][pl.ds(h, page, stride=N_HEADS), :].astype(BF16) for j in range(pps)], axis=0)
        s = lax.dot_general(qs, kt, (((1,), (1,)), ((), ())), preferred_element_type=F32)
        b_last = bias_ref[h, :, 0:page] * is_last
        s = jnp.concatenate([s[:, :(pps - 1) * page], s[:, (pps - 1) * page:] + b_last], axis=1)
        m, l, acc = _softmax_step(s, m_ref[h], l_ref[h], acc_ref[h], vt)
        m_ref[h] = m
        l_ref[h] = l
        acc_ref[h] = acc

    @pl.when(c == last)
    def _():
        lam = lam_ref[0:1, :]
        pad = jnp.zeros((page - n_new, HEAD_W), BF16)
        for h in range(N_HEADS):
            cols = slice(h * HEAD_W, (h + 1) * HEAD_W)
            qs = _stack_maps(q_ref[0, :, cols])
            kt = jnp.concatenate([kn_ref[0, :, cols].astype(BF16), pad], axis=0)
            vt = jnp.concatenate([vn_ref[0, :, cols].astype(BF16), pad], axis=0)
            s = lax.dot_general(qs, kt, (((1,), (1,)), ((), ())), preferred_element_type=F32)
            s = s + bias_ref[h, :, page:2 * page]
            _, l, acc = _softmax_step(s, m_ref[h], l_ref[h], acc_ref[h], vt)
            o_ref[0, :, cols] = _diff_finish(l, acc, lam, g_ref[...]).astype(o_ref.dtype)


def _decode_attention(page_table, cache_k, cache_v, q3, k3, v3, bias_d, lam, g):
    db, n_pages = page_table.shape
    n_pool, page = cache_k.shape[0], cache_k.shape[1]
    n_new = q3.shape[1]
    pps = min(DEC_PAGES_PER_STEP, n_pages)
    assert n_pages % pps == 0
    rows = page * N_HEADS
    ck = cache_k.reshape(n_pool * rows, HEAD_W)
    cv = cache_v.reshape(n_pool * rows, HEAD_W)
    pt = page_table.reshape(-1)

    def page_spec(j):
        return pl.BlockSpec((rows, HEAD_W), lambda b, c, pt: (pt[b * n_pages + c * pps + j], 0))

    tok = pl.BlockSpec((1, n_new, ATTN_W), lambda b, c, pt: (b, 0, 0))
    const = lambda a: pl.BlockSpec(a.shape, lambda b, c, pt: (0,) * a.ndim)
    grid_spec = pltpu.PrefetchScalarGridSpec(
        num_scalar_prefetch=1,
        grid=(db, n_pages // pps),
        in_specs=[page_spec(j) for j in range(pps)] * 2 + [tok, tok, tok, const(bias_d), const(lam), const(g)],
        out_specs=tok,
        scratch_shapes=[pltpu.VMEM((N_HEADS, 2 * n_new, 1), F32),
                        pltpu.VMEM((N_HEADS, 2 * n_new, 1), F32),
                        pltpu.VMEM((N_HEADS, 2 * n_new, HEAD_W), F32)],
    )
    return pl.pallas_call(
        functools.partial(_decode_attn_kernel, pages_per_step=pps, page=page),
        out_shape=jax.ShapeDtypeStruct((db, n_new, ATTN_W), BF16),
        grid_spec=grid_spec,
        compiler_params=pltpu.CompilerParams(
            dimension_semantics=("arbitrary", "arbitrary"), vmem_limit_bytes=VMEM_LIMIT_BYTES),
        name="decode_attention",
    )(pt, *([ck] * pps), *([cv] * pps), q3, k3, v3, bias_d, lam, g)


def _ln_silu(c, g, b):
    mu = jnp.mean(c, axis=-1, keepdims=True)
    xc = c - mu
    y = xc * lax.rsqrt(jnp.mean(xc * xc, axis=-1, keepdims=True) + EPS) * g + b
    return y * jax.nn.sigmoid(y)


CONV_HALO = 32
CONV_ROWS = 64


def _prompt_conv_kernel(prev_ref, cur_ref, w_ref, b_ref, g_ref, beta_ref, c_ref, buf_ref):
    i = pl.program_id(1)
    tc = cur_ref.shape[0]
    buf_ref[0:CONV_HALO, :] = jnp.where(i > 0, prev_ref[...], 0.0)
    buf_ref[CONV_HALO:, :] = cur_ref[...]
    off = CONV_HALO - (CONV_W - 1)
    for r in range(0, tc, CONV_ROWS):
        acc = jnp.broadcast_to(b_ref[...], (CONV_ROWS, b_ref.shape[1]))
        for j in range(CONV_W):
            acc = acc + w_ref[j:j + 1, :] * buf_ref[r + off + j:r + off + j + CONV_ROWS, :]
        c_ref[r:r + CONV_ROWS, :] = _ln_silu(acc, g_ref[...], beta_ref[...]).astype(c_ref.dtype)


def _prompt_conv(u, w, b, g, beta, batch, seq):
    ch = u.shape[1]
    tc = min(ROW_TILE, seq)
    nt = seq // tc
    hb = tc // CONV_HALO
    return pl.pallas_call(
        _prompt_conv_kernel,
        out_shape=jax.ShapeDtypeStruct((batch * seq, ch), BF16),
        grid=(batch, nt),
        in_specs=[pl.BlockSpec((CONV_HALO, ch), lambda bi, i: (jnp.maximum((bi * nt + i) * hb - 1, 0), 0)),
                  pl.BlockSpec((tc, ch), lambda bi, i: (bi * nt + i, 0)),
                  _const_spec(w.shape), _const_spec(b.shape), _const_spec(g.shape), _const_spec(beta.shape)],
        out_specs=pl.BlockSpec((tc, ch), lambda bi, i: (bi * nt + i, 0)),
        scratch_shapes=[pltpu.VMEM((tc + CONV_HALO, ch), F32)],
        compiler_params=pltpu.CompilerParams(dimension_semantics=("arbitrary", "arbitrary")),
        name="prompt_conv",
    )(u, u, w, b, g, beta)


def _sample_conv_kernel(st_ref, u_ref, w_ref, b_ref, g_ref, beta_ref, c_ref, ns_ref, buf_ref):
    hist = st_ref.shape[1]
    n_new = u_ref.shape[1]
    buf_ref[:, 0:hist, :] = st_ref[...]
    buf_ref[:, hist:hist + n_new, :] = u_ref[...]
    acc = jnp.broadcast_to(b_ref[...][None], u_ref.shape)
    for j in range(CONV_W):
        acc = acc + w_ref[j:j + 1, :][None] * buf_ref[:, j:j + n_new, :]
    c_ref[...] = _ln_silu(acc, g_ref[...][None], beta_ref[...][None]).astype(c_ref.dtype)
    ns_ref[...] = buf_ref[:, n_new:n_new + hist, :]


def _sample_conv(state, u3, w, b, g, beta):
    db, hist, ch = state.shape
    n_new = u3.shape[1]
    bb = min(16, db)
    blk = lambda r: pl.BlockSpec((bb, r, ch), lambda i: (i, 0, 0))
    return pl.pallas_call(
        _sample_conv_kernel,
        out_shape=[jax.ShapeDtypeStruct((db, n_new, ch), BF16),
                   jax.ShapeDtypeStruct((db, hist, ch), F32)],
        grid=(db // bb,),
        in_specs=[blk(hist), blk(n_new), _const_spec(w.shape), _const_spec(b.shape),
                  _const_spec(g.shape), _const_spec(beta.shape)],
        out_specs=[blk(n_new), blk(hist)],
        scratch_shapes=[pltpu.VMEM((bb, hist + n_new, ch), F32)],
        compiler_params=pltpu.CompilerParams(dimension_semantics=("arbitrary",)),
        name="sample_conv",
    )(state, u3, w, b, g, beta)


def _post_kernel(h_ref, o_ref, c_ref, p_ref, wo_ref, g2_ref, wgu_ref, wd_ref, gp_ref, wpg_ref, wpp_ref,
                 gf_ref, y_ref, acc_ref):
    a = o_ref.shape[1]
    mix = (jnp.dot(o_ref[...], wo_ref[0:a, :], preferred_element_type=F32)
           + jnp.dot(c_ref[...], wo_ref[a:, :], preferred_element_type=F32))
    h = h_ref[...] + mix
    h = _swiglu_half(h, g2_ref, wgu_ref, wd_ref, acc_ref)
    gate = jax.nn.sigmoid(jnp.dot(_rms(h, gp_ref[...]).astype(BF16), wpg_ref[...], preferred_element_type=F32))
    h = h + jnp.dot(p_ref[...].astype(BF16), wpp_ref[...], preferred_element_type=F32) * gate
    y_ref[...] = _rms(h, gf_ref[...])


def _post(h, o, c, p, wo, g2, wgu, wd, gp, wpg, wpp, gf):
    t, d = h.shape
    tm = min(ROW_TILE, t)
    row = lambda w: pl.BlockSpec((tm, w), lambda i: (i, 0))
    consts = [wo, g2, wgu, wd, gp, wpg, wpp, gf]
    return pl.pallas_call(
        _post_kernel,
        out_shape=jax.ShapeDtypeStruct((t, d), F32),
        grid=(t // tm,),
        in_specs=[row(d), row(o.shape[1]), row(c.shape[1]), row(p.shape[1])] + [_const_spec(a.shape) for a in consts],
        out_specs=row(d),
        scratch_shapes=[pltpu.VMEM((tm, d), F32)],
        compiler_params=pltpu.CompilerParams(
            dimension_semantics=("arbitrary",), vmem_limit_bytes=VMEM_LIMIT_BYTES),
        name="post_mix",
    )(h, o, c, p, *consts)


def _ffn_weights(wg, wu, wd):
    d, f = wg.shape
    n = f // FF_CHUNK
    wgu = jnp.concatenate([wg.reshape(d, n, FF_CHUNK), wu.reshape(d, n, FF_CHUNK)], axis=-1)
    return wgu.transpose(1, 0, 2).astype(BF16), wd.reshape(n, FF_CHUNK, d).astype(BF16)


def kernel(x_prompt, x_sample, cache_k, cache_v, state_conv, page_table, p_prompt, p_sample, rel_bias, ffn1_norm, ffn1_w_gate, ffn1_w_up, ffn1_w_down, mix_norm, w_in, lambda_q1, lambda_k1, lambda_q2, lambda_k2, attn_subln, conv_w, conv_b, conv_ln_g, conv_ln_b, w_out, ffn2_norm, ffn2_w_gate, ffn2_w_up, ffn2_w_down, ple_norm, w_ple_gate, w_ple_proj, final_norm):
    b, seq, d = x_prompt.shape
    db, n_new, _ = x_sample.shape
    assert cache_k.shape[0] == 1, "one layer"
    page = cache_k.shape[2]
    past = page_table.shape[1] * page

    wgu1, wd1 = _ffn_weights(ffn1_w_gate[0], ffn1_w_up[0], ffn1_w_down[0])
    wgu2, wd2 = _ffn_weights(ffn2_w_gate[0], ffn2_w_up[0], ffn2_w_down[0])
    win = w_in[0].astype(BF16)
    wo = w_out[0].astype(BF16)
    wpg = w_ple_gate[0].astype(BF16)
    wpp = w_ple_proj[0].astype(BF16)
    gf = final_norm.reshape(1, d)

    tq = min(ATT_TQ, seq)
    bias_p, bias_d, lam = _bias_tables(rel_bias, lambda_q1, lambda_k1, lambda_q2, lambda_k2,
                                       tq, tq, n_new, page, past)

    def rows(x):
        return _ffn_inproj(x.reshape(-1, d), ffn1_norm, wgu1, wd1, mix_norm, win)

    def post(h, o, c, p):
        return _post(h, o, c, p.reshape(-1, p.shape[-1]), wo, ffn2_norm, wgu2, wd2, ple_norm, wpg, wpp, gf)

    hs, qs, ks, vs, _, _, us = rows(x_sample)
    k3 = ks.reshape(db, n_new, ATTN_W)
    v3 = vs.reshape(db, n_new, ATTN_W)
    os_ = _decode_attention(page_table, cache_k[0], cache_v[0], qs.reshape(db, n_new, ATTN_W), k3, v3,
                            bias_d, lam, attn_subln)
    cs, conv_s = _sample_conv(state_conv[0], us.reshape(db, n_new, ATTN_W), conv_w[0], conv_b, conv_ln_g, conv_ln_b)
    y_s = post(hs, os_.reshape(db * n_new, ATTN_W), cs.reshape(db * n_new, -1), p_sample[0])

    hp, qp, kp, vp, kbp, vbp, up = rows(x_prompt)
    op = _prompt_attention(qp, kbp, vbp, bias_p, lam, attn_subln, b, seq)
    cp = _prompt_conv(up, conv_w[0], conv_b, conv_ln_g, conv_ln_b, b, seq)
    y_p = post(hp, op, cp, p_prompt[0])

    hist = CONV_W - 1
    conv_p = up.reshape(b, seq, -1)[:, seq - hist:]
    return (y_p.reshape(b, seq, d), y_s.reshape(db, n_new, d),
            kp.reshape(1, b, seq, N_HEADS, HEAD_W), vp.reshape(1, b, seq, N_HEADS, HEAD_W), conv_p[None],
            ks.reshape(1, db, n_new, N_HEADS, HEAD_W), vs.reshape(1, db, n_new, N_HEADS, HEAD_W), conv_s[None])
```

```python
import functools
import math

import jax
import jax.numpy as jnp
from jax import lax
from jax.experimental import pallas as pl
from jax.experimental.pallas import tpu as pltpu

N_HEADS = 4
HEAD_D = 64
HEAD_W = 2 * HEAD_D
ATTN_W = N_HEADS * HEAD_W
CONV_W = 31
N_BUCKETS = 32
MAX_EXACT = N_BUCKETS // 2
MAX_DIST = 128
EPS = 1e-6
NEG = -1e30
LAM_INIT = 0.8 - 0.6 * math.exp(-0.3 * 0)
LOG2E = math.log2(math.e)

LANES = 128
SUBLANES = 8
VMEM_LIMIT_BYTES = 56 * 1024 * 1024

FF_CHUNK = 256
ROW_TILE = 512
ATT_TILE = 256
DENOM_ROWS = 16
DEC_PAGES_PER_STEP = 16
CONV_HALO = 32
CONV_ROWS = 128
BF16 = jnp.bfloat16
F32 = jnp.float32


def _rms(x, g):
    return x * lax.rsqrt(jnp.mean(x * x, axis=-1, keepdims=True) + EPS) * g


def _const_spec(shape):
    nd = len(shape)
    return pl.BlockSpec(shape, lambda *_: (0,) * nd, pipeline_mode=pl.Buffered(1))


def _head_cols(h):
    return slice(h * HEAD_W, (h + 1) * HEAD_W)


def _swiglu_half(x, g_ref, wg_ref, wu_ref, wd_ref, acc_ref):
    u = _rms(x, g_ref[...]).astype(BF16)
    n_chunks = wg_ref.shape[1] // FF_CHUNK
    for c in range(n_chunks):
        cols = slice(c * FF_CHUNK, (c + 1) * FF_CHUNK)
        g = jnp.dot(u, wg_ref[:, cols], preferred_element_type=F32)
        up = jnp.dot(u, wu_ref[:, cols], preferred_element_type=F32)
        a = (g * jax.nn.sigmoid(g) * up).astype(BF16)
        d = jnp.dot(a, wd_ref[cols, :], preferred_element_type=F32)
        if c == 0:
            acc_ref[...] = d
        else:
            acc_ref[...] += d
    return x + 0.5 * acc_ref[...]


def _ffn_inproj_kernel(x_ref, g1_ref, wg_ref, wu_ref, wd_ref, gm_ref, win_ref,
                       h_ref, q_ref, k_ref, v_ref, kb_ref, vt_ref, u_ref, acc_ref):
    h = _swiglu_half(x_ref[...], g1_ref, wg_ref, wu_ref, wd_ref, acc_ref)
    h_ref[...] = h
    un = _rms(h, gm_ref[...]).astype(BF16)
    a = ATTN_W
    tm = x_ref.shape[0]
    q = jnp.dot(un, win_ref[:, 0:a], preferred_element_type=F32)
    q_ref[...] = (q * (HEAD_D ** -0.5 * LOG2E)).astype(BF16)
    k = jnp.dot(un, win_ref[:, a:2 * a], preferred_element_type=F32)
    kb_ref[...] = k.astype(BF16)
    v = jnp.dot(un, win_ref[:, 2 * a:3 * a], preferred_element_type=F32)
    for hd in range(N_HEADS):
        k_ref[pl.ds(hd, tm, stride=N_HEADS), :] = k[:, _head_cols(hd)]
        v_ref[pl.ds(hd, tm, stride=N_HEADS), :] = v[:, _head_cols(hd)]
    tk = vt_ref.shape[2]
    for s in range(vt_ref.shape[0]):
        vt_ref[s] = v[s * tk:(s + 1) * tk, :].T.astype(BF16)
    ga = jnp.dot(un, win_ref[:, 3 * a:4 * a], preferred_element_type=F32)
    gb = jnp.dot(un, win_ref[:, 4 * a:5 * a], preferred_element_type=F32)
    u_ref[...] = ga * jax.nn.sigmoid(gb)


def _ffn_inproj(x, g1, wg, wu, wd, gm, win, tk):
    t, d = x.shape
    tm = min(ROW_TILE, t)
    tk = min(tk, tm)
    assert t % tm == 0 and tm % tk == 0
    row = lambda w: pl.BlockSpec((tm, w), lambda i: (i, 0))
    outs = [
        jax.ShapeDtypeStruct((t, d), F32),
        jax.ShapeDtypeStruct((t, ATTN_W), BF16),
        jax.ShapeDtypeStruct((t * N_HEADS, HEAD_W), F32),
        jax.ShapeDtypeStruct((t * N_HEADS, HEAD_W), F32),
        jax.ShapeDtypeStruct((t, ATTN_W), BF16),
        jax.ShapeDtypeStruct((t // tk, ATTN_W, tk), BF16),
        jax.ShapeDtypeStruct((t, ATTN_W), F32),
    ]
    consts = [g1, wg, wu, wd, gm, win]
    return pl.pallas_call(
        _ffn_inproj_kernel,
        out_shape=outs,
        grid=(t // tm,),
        in_specs=[row(d)] + [_const_spec(c.shape) for c in consts],
        out_specs=[row(d), row(ATTN_W),
                   pl.BlockSpec((tm * N_HEADS, HEAD_W), lambda i: (i, 0)),
                   pl.BlockSpec((tm * N_HEADS, HEAD_W), lambda i: (i, 0)),
                   row(ATTN_W),
                   pl.BlockSpec((tm // tk, ATTN_W, tk), lambda i: (i, 0, 0)),
                   row(ATTN_W)],
        scratch_shapes=[pltpu.VMEM((tm, d), F32)],
        compiler_params=pltpu.CompilerParams(
            dimension_semantics=("arbitrary",), vmem_limit_bytes=VMEM_LIMIT_BYTES),
        name="ffn1_inproj",
    )(x, *consts)


def _bias_of_distance(dist, rb_ref, h):
    n = jnp.maximum(dist, 0)
    nf = jnp.maximum(n, MAX_EXACT).astype(F32)
    large = MAX_EXACT + (jnp.log(nf / MAX_EXACT) / math.log(MAX_DIST / MAX_EXACT)
                         * (N_BUCKETS - MAX_EXACT)).astype(jnp.int32)
    large = jnp.minimum(large, N_BUCKETS - 1)
    bucket = jnp.where(n < MAX_EXACT, n, large)
    far = rb_ref[N_BUCKETS - 1, h]
    val = jnp.zeros(dist.shape, F32)
    for i in range(N_BUCKETS - 1):
        val = jnp.where(bucket == i, rb_ref[i, h] - far, val)
    return jnp.where(dist >= 0, val * LOG2E, NEG)


def _bias_kernel(rb_ref, lq1_ref, lk1_ref, lq2_ref, lk2_ref, bp_ref, bd_ref, lam_ref):
    tk, tq = bp_ref.shape[2], bp_ref.shape[3]
    c = lax.broadcasted_iota(jnp.int32, (tk, tq), 0)
    r = lax.broadcasted_iota(jnp.int32, (tk, tq), 1)
    n_new = bd_ref.shape[1] // 2
    page = bd_ref.shape[2] // 2
    qi = lax.broadcasted_iota(jnp.int32, (2 * n_new, page), 0) % n_new
    cj = lax.broadcasted_iota(jnp.int32, (2 * n_new, page), 1)
    for h in range(N_HEADS):
        bp_ref[h, 0] = _bias_of_distance(tk + r - c, rb_ref, h)
        bp_ref[h, 1] = _bias_of_distance(r - c, rb_ref, h)
        bd_ref[h, :, 0:page] = _bias_of_distance(qi + page - cj, rb_ref, h)
        new = _bias_of_distance(qi - cj, rb_ref, h)
        bd_ref[h, :, page:2 * page] = jnp.where(cj < n_new, new, NEG)
    e1 = jnp.exp(jnp.sum(lq1_ref[...] * lk1_ref[...], axis=-1, keepdims=True))
    e2 = jnp.exp(jnp.sum(lq2_ref[...] * lk2_ref[...], axis=-1, keepdims=True))
    lam_ref[...] = jnp.broadcast_to(e1 - e2 + LAM_INIT, lam_ref.shape)


def _bias_tables(rel_bias, lq1, lk1, lq2, lk2, tile, n_new, page):
    vm = lambda shape: pl.BlockSpec(shape, lambda: (0,) * len(shape))
    shapes = [(N_HEADS, 2, tile, tile), (N_HEADS, 2 * n_new, 2 * page), (SUBLANES, max(LANES, tile))]
    return pl.pallas_call(
        _bias_kernel,
        out_shape=[jax.ShapeDtypeStruct(s, F32) for s in shapes],
        in_specs=[pl.BlockSpec(memory_space=pltpu.SMEM)] + [vm(lq1.shape)] * 4,
        out_specs=[vm(s) for s in shapes],
        name="bias_tables",
    )(rel_bias, lq1, lk1, lq2, lk2)


def _stack_maps(qh):
    lane = lax.broadcasted_iota(jnp.int32, qh.shape, 1)
    zero = jnp.zeros_like(qh)
    return jnp.concatenate([jnp.where(lane < HEAD_D, qh, zero),
                            jnp.where(lane < HEAD_D, zero, qh)], axis=0)


def _prompt_attn_kernel(q_ref, k_ref, vt_ref, bias_ref, lam_ref, gt_ref, o_ref, qs_ref, m_ref, acc_ref, s_ref):
    i = pl.program_id(1)
    tq = q_ref.shape[0]
    tk = bias_ref.shape[2]
    for h in range(N_HEADS):
        qs_ref[h] = _stack_maps(q_ref[:, _head_cols(h)])
    m_ref[...] = jnp.full(m_ref.shape, NEG, F32)
    acc_ref[...] = jnp.zeros(acc_ref.shape, F32)

    ones = jnp.ones((DENOM_ROWS, tk), BF16)

    def scores(j, h):
        start = pl.multiple_of(j * tk, tk)
        kt = k_ref[pl.ds(start, tk), _head_cols(h)]
        return lax.dot_general(kt, qs_ref[h], (((1,), (1,)), ((), ())), preferred_element_type=F32)

    def update(j, h, s, near):
        if near:
            b = bias_ref[h, j - i + 1]
            s = s + jnp.concatenate([b, b], axis=1)
        m_old = m_ref[h]
        m_new = jnp.maximum(m_old, jnp.max(s, axis=0, keepdims=True))
        p = jnp.exp2(s - m_new).astype(BF16)
        alpha = jnp.exp2(m_old - m_new)
        m_ref[h] = m_new
        vt = jnp.concatenate([vt_ref[j, _head_cols(h), :], ones], axis=0)
        acc_ref[h] = alpha * acc_ref[h] + jnp.dot(vt, p, preferred_element_type=F32)

    def step(j, near):
        nxt = jnp.minimum(j + 1, i)
        for h in range(N_HEADS):
            s = s_ref[h]
            s_ref[h] = scores(nxt, h)
            update(j, h, s, near)

    def far_body(j, carry):
        step(j, False)
        return carry

    def near_body(j, carry):
        step(j, True)
        return carry

    first_near = jnp.maximum(i - 1, 0)
    for h in range(N_HEADS):
        s_ref[h] = scores(0, h)
    lax.fori_loop(0, first_near, far_body, 0)
    lax.fori_loop(first_near, i + 1, near_body, 0)

    lam = lam_ref[0:1, 0:tq]
    for h in range(N_HEADS):
        acc = acc_ref[h, 0:HEAD_W, :]
        l = acc_ref[h, HEAD_W:HEAD_W + 1, :]
        o = acc[:, :tq] / l[:, :tq] - lam * (acc[:, tq:] / l[:, tq:])
        o = o * lax.rsqrt(jnp.mean(o * o, axis=0, keepdims=True) + EPS) * gt_ref[...] * (1.0 - LAM_INIT)
        o_ref[:, _head_cols(h)] = o.T.astype(o_ref.dtype)


def _prompt_attention(q, kb, vt, bias_p, lam, gt, batch, seq):
    tile = bias_p.shape[2]
    nq = seq // tile
    return pl.pallas_call(
        _prompt_attn_kernel,
        out_shape=jax.ShapeDtypeStruct((batch * seq, ATTN_W), BF16),
        grid=(batch, nq),
        in_specs=[pl.BlockSpec((tile, ATTN_W), lambda b, i: (b * nq + i, 0)),
                  pl.BlockSpec((seq, ATTN_W), lambda b, i: (b, 0)),
                  pl.BlockSpec((nq, ATTN_W, tile), lambda b, i: (b, 0, 0)),
                  _const_spec(bias_p.shape), _const_spec(lam.shape), _const_spec(gt.shape)],
        out_specs=pl.BlockSpec((tile, ATTN_W), lambda b, i: (b * nq + i, 0)),
        scratch_shapes=[pltpu.VMEM((N_HEADS, 2 * tile, HEAD_W), BF16),
                        pltpu.VMEM((N_HEADS, 1, 2 * tile), F32),
                        pltpu.VMEM((N_HEADS, HEAD_W + DENOM_ROWS, 2 * tile), F32),
                        pltpu.VMEM((N_HEADS, tile, 2 * tile), F32)],
        compiler_params=pltpu.CompilerParams(
            dimension_semantics=("arbitrary", "arbitrary"), vmem_limit_bytes=VMEM_LIMIT_BYTES),
        name="prompt_attention",
    )(q, kb, vt, bias_p, lam, gt)


def _softmax_step(s, m, l, acc, v):
    m_new = jnp.maximum(m, jnp.max(s, axis=-1, keepdims=True))
    p = jnp.exp2(s - m_new)
    alpha = jnp.exp2(m - m_new)
    l = alpha * l + jnp.sum(p, axis=-1, keepdims=True)
    acc = alpha * acc + jnp.dot(p.astype(BF16), v, preferred_element_type=F32)
    return m_new, l, acc


def _decode_attn_kernel(pt_ref, *refs, pages_per_step, page):
    del pt_ref
    pps = pages_per_step
    kp = refs[0:pps]
    vp = refs[pps:2 * pps]
    q_ref, kn_ref, vn_ref, bias_ref, lam_ref, g_ref, o_ref, m_ref, l_ref, acc_ref = refs[2 * pps:]
    c = pl.program_id(1)
    last = pl.num_programs(1) - 1
    n_new = q_ref.shape[1]

    @pl.when(c == 0)
    def _():
        m_ref[...] = jnp.full(m_ref.shape, NEG, F32)
        l_ref[...] = jnp.zeros(l_ref.shape, F32)
        acc_ref[...] = jnp.zeros(acc_ref.shape, F32)

    is_last = (c == last).astype(F32)
    for h in range(N_HEADS):
        qs = _stack_maps(q_ref[0, :, _head_cols(h)]).astype(BF16)
        kt = jnp.concatenate(
            [kp[j][pl.ds(h, page, stride=N_HEADS), :].astype(BF16) for j in range(pps)], axis=0)
        vt = jnp.concatenate(
            [vp[j][pl.ds(h, page, stride=N_HEADS), :].astype(BF16) for j in range(pps)], axis=0)
        s = lax.dot_general(qs, kt, (((1,), (1,)), ((), ())), preferred_element_type=F32)
        b_last = bias_ref[h, :, 0:page] * is_last
        s = jnp.concatenate([s[:, :(pps - 1) * page], s[:, (pps - 1) * page:] + b_last], axis=1)
        m, l, acc = _softmax_step(s, m_ref[h], l_ref[h], acc_ref[h], vt)
        m_ref[h] = m
        l_ref[h] = l
        acc_ref[h] = acc

    @pl.when(c == last)
    def _():
        lam = lam_ref[0:1, 0:HEAD_W]
        pad = jnp.zeros((page - n_new, HEAD_W), BF16)
        for h in range(N_HEADS):
            qs = _stack_maps(q_ref[0, :, _head_cols(h)]).astype(BF16)
            kn = kn_ref[pl.ds(h, n_new, stride=N_HEADS), :]
            vn = vn_ref[pl.ds(h, n_new, stride=N_HEADS), :]
            kt = jnp.concatenate([kn.astype(BF16), pad], axis=0)
            vt = jnp.concatenate([vn.astype(BF16), pad], axis=0)
            s = lax.dot_general(qs, kt, (((1,), (1,)), ((), ())), preferred_element_type=F32)
            s = s + bias_ref[h, :, page:2 * page]
            _, l, acc = _softmax_step(s, m_ref[h], l_ref[h], acc_ref[h], vt)
            o = acc[:n_new] / l[:n_new] - lam * (acc[n_new:] / l[n_new:])
            o_ref[0, :, _head_cols(h)] = (_rms(o, g_ref[...]) * (1.0 - LAM_INIT)).astype(o_ref.dtype)


def _decode_attention(page_table, cache_k, cache_v, q3, k_rows, v_rows, bias_d, lam, g):
    db, n_pages = page_table.shape
    n_pool, page = cache_k.shape[0], cache_k.shape[1]
    n_new = q3.shape[1]
    pps = min(DEC_PAGES_PER_STEP, n_pages)
    assert n_pages % pps == 0
    rows = page * N_HEADS
    ck = cache_k.reshape(n_pool * rows, HEAD_W)
    cv = cache_v.reshape(n_pool * rows, HEAD_W)
    pt = page_table.reshape(-1)

    def page_spec(j):
        return pl.BlockSpec((rows, HEAD_W), lambda b, c, pt: (pt[b * n_pages + c * pps + j], 0))

    tok = pl.BlockSpec((1, n_new, ATTN_W), lambda b, c, pt: (b, 0, 0))
    new_rows = pl.BlockSpec((n_new * N_HEADS, HEAD_W), lambda b, c, pt: (b, 0))
    const = lambda a: pl.BlockSpec(a.shape, lambda b, c, pt: (0,) * a.ndim)
    grid_spec = pltpu.PrefetchScalarGridSpec(
        num_scalar_prefetch=1,
        grid=(db, n_pages // pps),
        in_specs=([page_spec(j) for j in range(pps)] * 2
                  + [tok, new_rows, new_rows, const(bias_d), const(lam), const(g)]),
        out_specs=tok,
        scratch_shapes=[pltpu.VMEM((N_HEADS, 2 * n_new, 1), F32),
                        pltpu.VMEM((N_HEADS, 2 * n_new, 1), F32),
                        pltpu.VMEM((N_HEADS, 2 * n_new, HEAD_W), F32)],
    )
    return pl.pallas_call(
        functools.partial(_decode_attn_kernel, pages_per_step=pps, page=page),
        out_shape=jax.ShapeDtypeStruct((db, n_new, ATTN_W), BF16),
        grid_spec=grid_spec,
        compiler_params=pltpu.CompilerParams(
            dimension_semantics=("arbitrary", "arbitrary"), vmem_limit_bytes=VMEM_LIMIT_BYTES),
        name="decode_attention",
    )(pt, *([ck] * pps), *([cv] * pps), q3, k_rows, v_rows, bias_d, lam, g)


def _ln_silu(c, g, b):
    mu = jnp.mean(c, axis=-1, keepdims=True)
    xc = c - mu
    y = xc * lax.rsqrt(jnp.mean(xc * xc, axis=-1, keepdims=True) + EPS) * g + b
    return y * jax.nn.sigmoid(y)


def _conv_span():
    return CONV_ROWS + SUBLANES * (CONV_HALO // SUBLANES)


def _conv_rows(buf_ref, sh_ref, r0, w_ref, b_ref):
    off = CONV_HALO - (CONV_W - 1)
    span = _conv_span()
    for s in range(1, SUBLANES):
        sh_ref[s - 1] = buf_ref[r0 + s:r0 + s + span - SUBLANES, :]
    acc = jnp.broadcast_to(b_ref[...], (CONV_ROWS, b_ref.shape[1]))
    for j in range(CONV_W):
        a, s = divmod(off + j, SUBLANES)
        if s == 0:
            x = buf_ref[r0 + a * SUBLANES:r0 + a * SUBLANES + CONV_ROWS, :]
        else:
            x = sh_ref[s - 1, a * SUBLANES:a * SUBLANES + CONV_ROWS, :]
        acc = acc + w_ref[j:j + 1, :] * x
    return acc


def _sample_conv_kernel(st_ref, u_ref, w_ref, b_ref, g_ref, beta_ref, c_ref, ns_ref, buf_ref):
    hist = st_ref.shape[1]
    n_new = u_ref.shape[1]
    buf_ref[:, 0:hist, :] = st_ref[...]
    buf_ref[:, hist:hist + n_new, :] = u_ref[...]
    acc = jnp.broadcast_to(b_ref[...][None], u_ref.shape)
    for j in range(CONV_W):
        acc = acc + w_ref[j:j + 1, :][None] * buf_ref[:, j:j + n_new, :]
    c_ref[...] = _ln_silu(acc, g_ref[...][None], beta_ref[...][None]).astype(c_ref.dtype)
    ns_ref[...] = buf_ref[:, n_new:n_new + hist, :]


def _sample_conv(state, u3, w, b, g, beta):
    db, hist, ch = state.shape
    n_new = u3.shape[1]
    bb = min(16, db)
    blk = lambda r: pl.BlockSpec((bb, r, ch), lambda i: (i, 0, 0))
    return pl.pallas_call(
        _sample_conv_kernel,
        out_shape=[jax.ShapeDtypeStruct((db, n_new, ch), BF16),
                   jax.ShapeDtypeStruct((db, hist, ch), F32)],
        grid=(db // bb,),
        in_specs=[blk(hist), blk(n_new), _const_spec(w.shape), _const_spec(b.shape),
                  _const_spec(g.shape), _const_spec(beta.shape)],
        out_specs=[blk(n_new), blk(hist)],
        scratch_shapes=[pltpu.VMEM((bb, hist + n_new, ch), F32)],
        compiler_params=pltpu.CompilerParams(dimension_semantics=("arbitrary",)),
        name="sample_conv",
    )(state, u3, w, b, g, beta)


def _post_body(h, o, c, p_ref, wo_ref, g2_ref, wg_ref, wu_ref, wd_ref, gp_ref, wpg_ref, wpp_ref, gf_ref,
               y_ref, acc_ref):
    a = o.shape[1]
    mix = (jnp.dot(o, wo_ref[0:a, :], preferred_element_type=F32)
           + jnp.dot(c, wo_ref[a:, :], preferred_element_type=F32))
    h = _swiglu_half(h + mix, g2_ref, wg_ref, wu_ref, wd_ref, acc_ref)
    gate = jax.nn.sigmoid(jnp.dot(_rms(h, gp_ref[...]).astype(BF16), wpg_ref[...], preferred_element_type=F32))
    h = h + jnp.dot(p_ref[...].astype(BF16), wpp_ref[...], preferred_element_type=F32) * gate
    y_ref[...] = _rms(h, gf_ref[...])


def _post_kernel(h_ref, o_ref, c_ref, p_ref, *rest):
    _post_body(h_ref[...], o_ref[...], c_ref[...], p_ref, *rest)


def _post_conv_kernel(h_ref, o_ref, uprev_ref, u_ref, p_ref, cw_ref, cb_ref, cg_ref, cbeta_ref, *rest,
                      tiles_per_seq):
    *rest, buf_ref, sh_ref, c_ref = rest
    i = pl.program_id(0) % tiles_per_seq
    tm = u_ref.shape[0]
    buf_ref[0:CONV_HALO, :] = jnp.where(i > 0, uprev_ref[...], 0.0)
    buf_ref[CONV_HALO:, :] = u_ref[...]
    for r0 in range(0, tm, CONV_ROWS):
        acc = _conv_rows(buf_ref, sh_ref, r0, cw_ref, cb_ref)
        c_ref[r0:r0 + CONV_ROWS, :] = _ln_silu(acc, cg_ref[...], cbeta_ref[...]).astype(c_ref.dtype)
    _post_body(h_ref[...], o_ref[...], c_ref[...], p_ref, *rest)


def _post(h, o, c_or_u, p, conv_params, consts, seq=None):
    t, d = h.shape
    tm = min(ROW_TILE, t)
    row = lambda w: pl.BlockSpec((tm, w), lambda i: (i, 0))
    ch = c_or_u.shape[1]
    common = dict(
        out_shape=jax.ShapeDtypeStruct((t, d), F32),
        grid=(t // tm,),
        out_specs=row(d),
        compiler_params=pltpu.CompilerParams(
            dimension_semantics=("arbitrary",), vmem_limit_bytes=VMEM_LIMIT_BYTES),
    )
    const_specs = [_const_spec(a.shape) for a in consts]
    acc = pltpu.VMEM((tm, d), F32)
    if conv_params is None:
        return pl.pallas_call(
            _post_kernel,
            in_specs=[row(d), row(o.shape[1]), row(ch), row(p.shape[1])] + const_specs,
            scratch_shapes=[acc], name="post_mix", **common,
        )(h, o, c_or_u, p, *consts)
    assert seq % tm == 0 and tm % CONV_ROWS == 0
    hb = tm // CONV_HALO
    return pl.pallas_call(
        functools.partial(_post_conv_kernel, tiles_per_seq=seq // tm),
        in_specs=[row(d), row(o.shape[1]),
                  pl.BlockSpec((CONV_HALO, ch), lambda i: (jnp.maximum(i * hb - 1, 0), 0)),
                  row(ch), row(p.shape[1])] + [_const_spec(a.shape) for a in conv_params] + const_specs,
        scratch_shapes=[acc, pltpu.VMEM((tm + CONV_HALO, ch), F32),
                        pltpu.VMEM((SUBLANES - 1, _conv_span() - SUBLANES, ch), F32),
                        pltpu.VMEM((tm, ch), BF16)],
        name="post_conv_mix", **common,
    )(h, o, c_or_u, c_or_u, p, *conv_params, *consts)


def kernel(x_prompt, x_sample, cache_k, cache_v, state_conv, page_table, p_prompt, p_sample, rel_bias, ffn1_norm, ffn1_w_gate, ffn1_w_up, ffn1_w_down, mix_norm, w_in, lambda_q1, lambda_k1, lambda_q2, lambda_k2, attn_subln, conv_w, conv_b, conv_ln_g, conv_ln_b, w_out, ffn2_norm, ffn2_w_gate, ffn2_w_up, ffn2_w_down, ple_norm, w_ple_gate, w_ple_proj, final_norm):
    b, seq, d = x_prompt.shape
    db, n_new, _ = x_sample.shape
    assert cache_k.shape[0] == 1, "one layer"
    page = cache_k.shape[2]
    tile = min(ATT_TILE, seq)
    bf = lambda w: w[0].astype(BF16)

    bias_p, bias_d, lam = _bias_tables(rel_bias, lambda_q1, lambda_k1, lambda_q2, lambda_k2, tile, n_new, page)
    ffn1 = (ffn1_norm, bf(ffn1_w_gate), bf(ffn1_w_up), bf(ffn1_w_down), mix_norm, bf(w_in))
    post_consts = (bf(w_out), ffn2_norm, bf(ffn2_w_gate), bf(ffn2_w_up), bf(ffn2_w_down), ple_norm,
                   bf(w_ple_gate), bf(w_ple_proj), final_norm.reshape(1, d))
    conv_params = (conv_w[0], conv_b, conv_ln_g, conv_ln_b)
    kv5 = lambda a, n, s: a.reshape(1, n, s, N_HEADS, HEAD_W)

    hs, qs, ks, vs, _, _, us = _ffn_inproj(x_sample.reshape(-1, d), *ffn1, tile)
    os_ = _decode_attention(page_table, cache_k[0], cache_v[0], qs.astype(F32).reshape(db, n_new, ATTN_W),
                            ks, vs, bias_d, lam, attn_subln)
    cs, conv_s = _sample_conv(state_conv[0], us.reshape(db, n_new, -1), *conv_params)
    y_s = _post(hs, os_.reshape(db * n_new, ATTN_W), cs.reshape(db * n_new, -1),
                p_sample[0].reshape(db * n_new, -1), None, post_consts)

    hp, qp, kp, vp, kbp, vtp, up = _ffn_inproj(x_prompt.reshape(-1, d), *ffn1, tile)
    op = _prompt_attention(qp, kbp, vtp, bias_p, lam, attn_subln.reshape(HEAD_W, 1), b, seq)
    y_p = _post(hp, op, up, p_prompt[0].reshape(b * seq, -1), conv_params, post_consts, seq)

    hist = CONV_W - 1
    conv_p = up.reshape(b, seq, -1)[:, seq - hist:]
    return (y_p.reshape(b, seq, d), y_s.reshape(db, n_new, d),
            kv5(kp, b, seq), kv5(vp, b, seq), conv_p[None],
            kv5(ks, db, n_new), kv5(vs, db, n_new), conv_s[None])
```

```python
import functools
import math

import jax
import jax.numpy as jnp
from jax import lax
from jax.experimental import pallas as pl
from jax.experimental.pallas import tpu as pltpu

N_HEADS = 4
HEAD_D = 64
HEAD_W = 2 * HEAD_D
ATTN_W = N_HEADS * HEAD_W
CONV_W = 31
N_BUCKETS = 32
MAX_EXACT = N_BUCKETS // 2
MAX_DIST = 128
EPS = 1e-6
NEG = -1e30
LAM_INIT = 0.8 - 0.6 * math.exp(-0.3 * 0)
LOG2E = math.log2(math.e)

LANES = 128
SUBLANES = 8
VMEM_LIMIT_BYTES = 56 * 1024 * 1024

FF_CHUNK = 256
ROW_TILE = 512
ATT_TQ = 512
ATT_TK = 256
DENOM_ROWS = 16
DEC_PAGES_PER_STEP = 16
CONV_HALO = 32
CONV_ROWS = 128
BF16 = jnp.bfloat16
F32 = jnp.float32


def _rms(x, g):
    return x * lax.rsqrt(jnp.mean(x * x, axis=-1, keepdims=True) + EPS) * g


def _const_spec(shape):
    nd = len(shape)
    return pl.BlockSpec(shape, lambda *_: (0,) * nd, pipeline_mode=pl.Buffered(1))


def _head_cols(h):
    return slice(h * HEAD_W, (h + 1) * HEAD_W)


def _swiglu_half(x, g_ref, wg_ref, wu_ref, wd_ref, acc_ref):
    u = _rms(x, g_ref[...]).astype(BF16)
    n_chunks = wg_ref.shape[1] // FF_CHUNK
    for c in range(n_chunks):
        cols = slice(c * FF_CHUNK, (c + 1) * FF_CHUNK)
        g = jnp.dot(u, wg_ref[:, cols], preferred_element_type=F32)
        up = jnp.dot(u, wu_ref[:, cols], preferred_element_type=F32)
        a = (g * jax.nn.sigmoid(g) * up).astype(BF16)
        d = jnp.dot(a, wd_ref[cols, :], preferred_element_type=F32)
        if c == 0:
            acc_ref[...] = d
        else:
            acc_ref[...] += d
    return x + 0.5 * acc_ref[...]


def _ffn_inproj_kernel(x_ref, g1_ref, wg_ref, wu_ref, wd_ref, gm_ref, win_ref,
                       h_ref, q_ref, k_ref, v_ref, kb_ref, vt_ref, u_ref, acc_ref):
    h = _swiglu_half(x_ref[...], g1_ref, wg_ref, wu_ref, wd_ref, acc_ref)
    h_ref[...] = h
    un = _rms(h, gm_ref[...]).astype(BF16)
    a = ATTN_W
    tm = x_ref.shape[0]
    q = jnp.dot(un, win_ref[:, 0:a], preferred_element_type=F32)
    q_ref[...] = (q * (HEAD_D ** -0.5 * LOG2E)).astype(BF16)
    k = jnp.dot(un, win_ref[:, a:2 * a], preferred_element_type=F32)
    kb_ref[...] = k.astype(BF16)
    v = jnp.dot(un, win_ref[:, 2 * a:3 * a], preferred_element_type=F32)
    for hd in range(N_HEADS):
        k_ref[pl.ds(hd, tm, stride=N_HEADS), :] = k[:, _head_cols(hd)]
        v_ref[pl.ds(hd, tm, stride=N_HEADS), :] = v[:, _head_cols(hd)]
    tk = vt_ref.shape[2]
    for s in range(vt_ref.shape[0]):
        vt_ref[s] = v[s * tk:(s + 1) * tk, :].T.astype(BF16)
    ga = jnp.dot(un, win_ref[:, 3 * a:4 * a], preferred_element_type=F32)
    gb = jnp.dot(un, win_ref[:, 4 * a:5 * a], preferred_element_type=F32)
    u_ref[...] = ga * jax.nn.sigmoid(gb)


def _ffn_inproj(x, g1, wg, wu, wd, gm, win, tk):
    t, d = x.shape
    tm = min(ROW_TILE, t)
    tk = min(tk, tm)
    assert t % tm == 0 and tm % tk == 0
    row = lambda w: pl.BlockSpec((tm, w), lambda i: (i, 0))
    outs = [
        jax.ShapeDtypeStruct((t, d), F32),
        jax.ShapeDtypeStruct((t, ATTN_W), BF16),
        jax.ShapeDtypeStruct((t * N_HEADS, HEAD_W), F32),
        jax.ShapeDtypeStruct((t * N_HEADS, HEAD_W), F32),
        jax.ShapeDtypeStruct((t, ATTN_W), BF16),
        jax.ShapeDtypeStruct((t // tk, ATTN_W, tk), BF16),
        jax.ShapeDtypeStruct((t, ATTN_W), F32),
    ]
    consts = [g1, wg, wu, wd, gm, win]
    return pl.pallas_call(
        _ffn_inproj_kernel,
        out_shape=outs,
        grid=(t // tm,),
        in_specs=[row(d)] + [_const_spec(c.shape) for c in consts],
        out_specs=[row(d), row(ATTN_W),
                   pl.BlockSpec((tm * N_HEADS, HEAD_W), lambda i: (i, 0)),
                   pl.BlockSpec((tm * N_HEADS, HEAD_W), lambda i: (i, 0)),
                   row(ATTN_W),
                   pl.BlockSpec((tm // tk, ATTN_W, tk), lambda i: (i, 0, 0)),
                   row(ATTN_W)],
        scratch_shapes=[pltpu.VMEM((tm, d), F32)],
        compiler_params=pltpu.CompilerParams(
            dimension_semantics=("arbitrary",), vmem_limit_bytes=VMEM_LIMIT_BYTES),
        name="ffn1_inproj",
    )(x, *consts)


def _bias_of_distance(dist, rb_ref, h):
    n = jnp.maximum(dist, 0)
    nf = jnp.maximum(n, MAX_EXACT).astype(F32)
    large = MAX_EXACT + (jnp.log(nf / MAX_EXACT) / math.log(MAX_DIST / MAX_EXACT)
                         * (N_BUCKETS - MAX_EXACT)).astype(jnp.int32)
    large = jnp.minimum(large, N_BUCKETS - 1)
    bucket = jnp.where(n < MAX_EXACT, n, large)
    far = rb_ref[N_BUCKETS - 1, h]
    val = jnp.zeros(dist.shape, F32)
    for i in range(N_BUCKETS - 1):
        val = jnp.where(bucket == i, rb_ref[i, h] - far, val)
    return jnp.where(dist >= 0, val * LOG2E, NEG)


def _bias_kernel(rb_ref, lq1_ref, lk1_ref, lq2_ref, lk2_ref, bp_ref, bd_ref, lam_ref):
    tk, tq = bp_ref.shape[2], bp_ref.shape[3]
    c = lax.broadcasted_iota(jnp.int32, (tk, tq), 0)
    r = lax.broadcasted_iota(jnp.int32, (tk, tq), 1)
    n_new = bd_ref.shape[1] // 2
    page = bd_ref.shape[2] // 2
    qi = lax.broadcasted_iota(jnp.int32, (2 * n_new, page), 0) % n_new
    cj = lax.broadcasted_iota(jnp.int32, (2 * n_new, page), 1)
    for h in range(N_HEADS):
        for n in range(bp_ref.shape[1]):
            bp_ref[h, n] = _bias_of_distance(r - c - (n - 1) * tk, rb_ref, h)
        bd_ref[h, :, 0:page] = _bias_of_distance(qi + page - cj, rb_ref, h)
        new = _bias_of_distance(qi - cj, rb_ref, h)
        bd_ref[h, :, page:2 * page] = jnp.where(cj < n_new, new, NEG)
    e1 = jnp.exp(jnp.sum(lq1_ref[...] * lk1_ref[...], axis=-1, keepdims=True))
    e2 = jnp.exp(jnp.sum(lq2_ref[...] * lk2_ref[...], axis=-1, keepdims=True))
    lam_ref[...] = jnp.broadcast_to(e1 - e2 + LAM_INIT, lam_ref.shape)


def _bias_tables(rel_bias, lq1, lk1, lq2, lk2, tq, tk, n_new, page):
    vm = lambda shape: pl.BlockSpec(shape, lambda: (0,) * len(shape))
    assert tq % tk == 0 and tk >= MAX_DIST
    shapes = [(N_HEADS, tq // tk + 1, tk, tq), (N_HEADS, 2 * n_new, 2 * page), (SUBLANES, max(LANES, tq))]
    return pl.pallas_call(
        _bias_kernel,
        out_shape=[jax.ShapeDtypeStruct(s, F32) for s in shapes],
        in_specs=[pl.BlockSpec(memory_space=pltpu.SMEM)] + [vm(lq1.shape)] * 4,
        out_specs=[vm(s) for s in shapes],
        name="bias_tables",
    )(rel_bias, lq1, lk1, lq2, lk2)


def _stack_maps(qh):
    lane = lax.broadcasted_iota(jnp.int32, qh.shape, 1)
    zero = jnp.zeros_like(qh)
    return jnp.concatenate([jnp.where(lane < HEAD_D, qh, zero),
                            jnp.where(lane < HEAD_D, zero, qh)], axis=0)


def _prompt_attn_kernel(q_ref, k_ref, vt_ref, bias_ref, lam_ref, gt_ref, o_ref, qs_ref, m_ref, acc_ref, s_ref):
    tq = q_ref.shape[0]
    tk = bias_ref.shape[2]
    first_key_tile = pl.program_id(1) * (tq // tk)
    last_key_tile = first_key_tile + tq // tk - 1
    for h in range(N_HEADS):
        qs_ref[h] = _stack_maps(q_ref[:, _head_cols(h)])
    m_ref[...] = jnp.full(m_ref.shape, NEG, F32)
    acc_ref[...] = jnp.zeros(acc_ref.shape, F32)

    ones = jnp.ones((DENOM_ROWS, tk), BF16)

    def scores(j, h):
        start = pl.multiple_of(j * tk, tk)
        kt = k_ref[pl.ds(start, tk), _head_cols(h)]
        return lax.dot_general(kt, qs_ref[h], (((1,), (1,)), ((), ())), preferred_element_type=F32)

    def update(j, h, s, near):
        if near:
            b = bias_ref[h, j - first_key_tile + 1]
            s = s + jnp.concatenate([b, b], axis=1)
        m_old = m_ref[h]
        m_new = jnp.maximum(m_old, jnp.max(s, axis=0, keepdims=True))
        p = jnp.exp2(s - m_new).astype(BF16)
        alpha = jnp.exp2(m_old - m_new)
        m_ref[h] = m_new
        vt = jnp.concatenate([vt_ref[j, _head_cols(h), :], ones], axis=0)
        acc_ref[h] = alpha * acc_ref[h] + jnp.dot(vt, p, preferred_element_type=F32)

    def step(j, near):
        nxt = jnp.minimum(j + 1, last_key_tile)
        for h in range(N_HEADS):
            s = s_ref[h]
            s_ref[h] = scores(nxt, h)
            update(j, h, s, near)

    def far_body(j, carry):
        step(j, False)
        return carry

    def near_body(j, carry):
        step(j, True)
        return carry

    first_near = jnp.maximum(first_key_tile - 1, 0)
    for h in range(N_HEADS):
        s_ref[h] = scores(0, h)
    lax.fori_loop(0, first_near, far_body, 0)
    lax.fori_loop(first_near, last_key_tile + 1, near_body, 0)

    lam = lam_ref[0:1, 0:tq]
    for h in range(N_HEADS):
        acc = acc_ref[h, 0:HEAD_W, :]
        l = acc_ref[h, HEAD_W:HEAD_W + 1, :]
        o = acc[:, :tq] / l[:, :tq] - lam * (acc[:, tq:] / l[:, tq:])
        o = o * lax.rsqrt(jnp.mean(o * o, axis=0, keepdims=True) + EPS) * gt_ref[...] * (1.0 - LAM_INIT)
        o_ref[:, _head_cols(h)] = o.T.astype(o_ref.dtype)


def _prompt_attention(q, kb, vt, bias_p, lam, gt, batch, seq):
    tk, tq = bias_p.shape[2], bias_p.shape[3]
    nq = seq // tq
    nk = seq // tk
    return pl.pallas_call(
        _prompt_attn_kernel,
        out_shape=jax.ShapeDtypeStruct((batch * seq, ATTN_W), BF16),
        grid=(batch, nq),
        in_specs=[pl.BlockSpec((tq, ATTN_W), lambda b, i: (b * nq + i, 0)),
                  pl.BlockSpec((seq, ATTN_W), lambda b, i: (b, 0)),
                  pl.BlockSpec((nk, ATTN_W, tk), lambda b, i: (b, 0, 0)),
                  _const_spec(bias_p.shape), _const_spec(lam.shape), _const_spec(gt.shape)],
        out_specs=pl.BlockSpec((tq, ATTN_W), lambda b, i: (b * nq + i, 0)),
        scratch_shapes=[pltpu.VMEM((N_HEADS, 2 * tq, HEAD_W), BF16),
                        pltpu.VMEM((N_HEADS, 1, 2 * tq), F32),
                        pltpu.VMEM((N_HEADS, HEAD_W + DENOM_ROWS, 2 * tq), F32),
                        pltpu.VMEM((N_HEADS, tk, 2 * tq), F32)],
        compiler_params=pltpu.CompilerParams(
            dimension_semantics=("arbitrary", "arbitrary"), vmem_limit_bytes=VMEM_LIMIT_BYTES),
        name="prompt_attention",
    )(q, kb, vt, bias_p, lam, gt)


def _softmax_step(s, m, l, acc, v):
    m_new = jnp.maximum(m, jnp.max(s, axis=-1, keepdims=True))
    p = jnp.exp2(s - m_new)
    alpha = jnp.exp2(m - m_new)
    l = alpha * l + jnp.sum(p, axis=-1, keepdims=True)
    acc = alpha * acc + jnp.dot(p.astype(BF16), v, preferred_element_type=F32)
    return m_new, l, acc


def _decode_attn_kernel(pt_ref, ck_hbm, cv_hbm, q_ref, kn_ref, vn_ref, bias_ref, lam_ref, g_ref, o_ref,
                        kbuf, vbuf, sem, *, chunk_pages, page):
    b = pl.program_id(0)
    nb = pl.num_programs(0)
    n_chunks = kbuf.shape[0]
    rows = page * N_HEADS
    n_new = q_ref.shape[1]

    def chunk_copies(seq, c):
        copies = []
        for j in range(chunk_pages):
            pg = pt_ref[(seq * n_chunks + c) * chunk_pages + j]
            src = pl.ds(pl.multiple_of(pg * rows, rows), rows)
            dst = pl.ds(j * rows, rows)
            copies.append(pltpu.make_async_copy(ck_hbm.at[src], kbuf.at[c, dst], sem.at[0, c]))
            copies.append(pltpu.make_async_copy(cv_hbm.at[src], vbuf.at[c, dst], sem.at[1, c]))
        return copies

    @pl.when(b == 0)
    def _():
        for c in range(n_chunks):
            for cp in chunk_copies(0, c):
                cp.start()

    nxt = jnp.minimum(b + 1, nb - 1)
    qs = [_stack_maps(q_ref[0, :, _head_cols(h)]).astype(BF16) for h in range(N_HEADS)]
    state = [(jnp.full((2 * n_new, 1), NEG, F32), jnp.zeros((2 * n_new, 1), F32),
              jnp.zeros((2 * n_new, HEAD_W), F32)) for _ in range(N_HEADS)]
    for c in range(n_chunks):
        for cp in chunk_copies(b, c):
            cp.wait()
        for h in range(N_HEADS):
            kt = jnp.concatenate(
                [kbuf.at[c][pl.ds(j * rows + h, page, stride=N_HEADS), :].astype(BF16)
                 for j in range(chunk_pages)], axis=0)
            vt = jnp.concatenate(
                [vbuf.at[c][pl.ds(j * rows + h, page, stride=N_HEADS), :].astype(BF16)
                 for j in range(chunk_pages)], axis=0)
            s = lax.dot_general(qs[h], kt, (((1,), (1,)), ((), ())), preferred_element_type=F32)
            if c == n_chunks - 1:
                near = (chunk_pages - 1) * page
                s = jnp.concatenate([s[:, :near], s[:, near:] + bias_ref[h, :, 0:page]], axis=1)
            state[h] = _softmax_step(s, *state[h], vt)
        for cp in chunk_copies(nxt, c):
            cp.start()

    lam = lam_ref[0:1, 0:HEAD_W]
    pad = jnp.zeros((page - n_new, HEAD_W), BF16)
    for h in range(N_HEADS):
        kn = kn_ref[pl.ds(h, n_new, stride=N_HEADS), :]
        vn = vn_ref[pl.ds(h, n_new, stride=N_HEADS), :]
        kt = jnp.concatenate([kn.astype(BF16), pad], axis=0)
        vt = jnp.concatenate([vn.astype(BF16), pad], axis=0)
        s = lax.dot_general(qs[h], kt, (((1,), (1,)), ((), ())), preferred_element_type=F32)
        s = s + bias_ref[h, :, page:2 * page]
        _, l, acc = _softmax_step(s, *state[h], vt)
        o = acc[:n_new] / l[:n_new] - lam * (acc[n_new:] / l[n_new:])
        o_ref[0, :, _head_cols(h)] = (_rms(o, g_ref[...]) * (1.0 - LAM_INIT)).astype(o_ref.dtype)

    @pl.when(b == nb - 1)
    def _():
        for c in range(n_chunks):
            for cp in chunk_copies(nxt, c):
                cp.wait()


def _decode_attention(page_table, cache_k, cache_v, q3, k_rows, v_rows, bias_d, lam, g):
    db, n_pages = page_table.shape
    n_pool, page = cache_k.shape[0], cache_k.shape[1]
    n_new = q3.shape[1]
    pps = min(DEC_PAGES_PER_STEP, n_pages)
    assert n_pages % pps == 0
    n_chunks = n_pages // pps
    rows = page * N_HEADS
    ck = cache_k.reshape(n_pool * rows, HEAD_W)
    cv = cache_v.reshape(n_pool * rows, HEAD_W)
    pt = page_table.reshape(-1)

    tok = pl.BlockSpec((1, n_new, ATTN_W), lambda b, pt: (b, 0, 0))
    new_rows = pl.BlockSpec((n_new * N_HEADS, HEAD_W), lambda b, pt: (b, 0))
    const = lambda a: pl.BlockSpec(a.shape, lambda b, pt: (0,) * a.ndim)
    hbm = pl.BlockSpec(memory_space=pl.ANY)
    grid_spec = pltpu.PrefetchScalarGridSpec(
        num_scalar_prefetch=1,
        grid=(db,),
        in_specs=[hbm, hbm, tok, new_rows, new_rows, const(bias_d), const(lam), const(g)],
        out_specs=tok,
        scratch_shapes=[pltpu.VMEM((n_chunks, pps * rows, HEAD_W), F32),
                        pltpu.VMEM((n_chunks, pps * rows, HEAD_W), F32),
                        pltpu.SemaphoreType.DMA((2, n_chunks))],
    )
    return pl.pallas_call(
        functools.partial(_decode_attn_kernel, chunk_pages=pps, page=page),
        out_shape=jax.ShapeDtypeStruct((db, n_new, ATTN_W), BF16),
        grid_spec=grid_spec,
        compiler_params=pltpu.CompilerParams(
            dimension_semantics=("arbitrary",), vmem_limit_bytes=VMEM_LIMIT_BYTES),
        name="decode_attention",
    )(pt, ck, cv, q3, k_rows, v_rows, bias_d, lam, g)


def _ln_silu(c, g, b):
    mu = jnp.mean(c, axis=-1, keepdims=True)
    xc = c - mu
    y = xc * lax.rsqrt(jnp.mean(xc * xc, axis=-1, keepdims=True) + EPS) * g + b
    return y * jax.nn.sigmoid(y)


def _conv_span():
    return CONV_ROWS + SUBLANES * (CONV_HALO // SUBLANES)


def _conv_rows(buf_ref, sh_ref, r0, w_ref, b_ref):
    off = CONV_HALO - (CONV_W - 1)
    span = _conv_span()
    for s in range(1, SUBLANES):
        sh_ref[s - 1] = buf_ref[r0 + s:r0 + s + span - SUBLANES, :]
    acc = jnp.broadcast_to(b_ref[...], (CONV_ROWS, b_ref.shape[1]))
    for j in range(CONV_W):
        a, s = divmod(off + j, SUBLANES)
        if s == 0:
            x = buf_ref[r0 + a * SUBLANES:r0 + a * SUBLANES + CONV_ROWS, :]
        else:
            x = sh_ref[s - 1, a * SUBLANES:a * SUBLANES + CONV_ROWS, :]
        acc = acc + w_ref[j:j + 1, :] * x
    return acc


def _sample_conv_kernel(st_ref, u_ref, w_ref, b_ref, g_ref, beta_ref, c_ref, ns_ref, buf_ref):
    hist = st_ref.shape[1]
    n_new = u_ref.shape[1]
    buf_ref[:, 0:hist, :] = st_ref[...]
    buf_ref[:, hist:hist + n_new, :] = u_ref[...]
    acc = jnp.broadcast_to(b_ref[...][None], u_ref.shape)
    for j in range(CONV_W):
        acc = acc + w_ref[j:j + 1, :][None] * buf_ref[:, j:j + n_new, :]
    c_ref[...] = _ln_silu(acc, g_ref[...][None], beta_ref[...][None]).astype(c_ref.dtype)
    ns_ref[...] = buf_ref[:, n_new:n_new + hist, :]


def _sample_conv(state, u3, w, b, g, beta):
    db, hist, ch = state.shape
    n_new = u3.shape[1]
    bb = min(16, db)
    blk = lambda r: pl.BlockSpec((bb, r, ch), lambda i: (i, 0, 0))
    return pl.pallas_call(
        _sample_conv_kernel,
        out_shape=[jax.ShapeDtypeStruct((db, n_new, ch), BF16),
                   jax.ShapeDtypeStruct((db, hist, ch), F32)],
        grid=(db // bb,),
        in_specs=[blk(hist), blk(n_new), _const_spec(w.shape), _const_spec(b.shape),
                  _const_spec(g.shape), _const_spec(beta.shape)],
        out_specs=[blk(n_new), blk(hist)],
        scratch_shapes=[pltpu.VMEM((bb, hist + n_new, ch), F32)],
        compiler_params=pltpu.CompilerParams(dimension_semantics=("arbitrary",)),
        name="sample_conv",
    )(state, u3, w, b, g, beta)


def _post_body(h, o, c, p_ref, wo_ref, g2_ref, wg_ref, wu_ref, wd_ref, gp_ref, wpg_ref, wpp_ref, gf_ref,
               y_ref, acc_ref):
    a = o.shape[1]
    mix = (jnp.dot(o, wo_ref[0:a, :], preferred_element_type=F32)
           + jnp.dot(c, wo_ref[a:, :], preferred_element_type=F32))
    h = _swiglu_half(h + mix, g2_ref, wg_ref, wu_ref, wd_ref, acc_ref)
    gate = jax.nn.sigmoid(jnp.dot(_rms(h, gp_ref[...]).astype(BF16), wpg_ref[...], preferred_element_type=F32))
    h = h + jnp.dot(p_ref[...].astype(BF16), wpp_ref[...], preferred_element_type=F32) * gate
    y_ref[...] = _rms(h, gf_ref[...])


def _post_kernel(h_ref, o_ref, c_ref, p_ref, *rest):
    _post_body(h_ref[...], o_ref[...], c_ref[...], p_ref, *rest)


def _post_conv_kernel(h_ref, o_ref, uprev_ref, u_ref, p_ref, cw_ref, cb_ref, cg_ref, cbeta_ref, *rest,
                      tiles_per_seq):
    *rest, buf_ref, sh_ref, c_ref = rest
    i = pl.program_id(0) % tiles_per_seq
    tm = u_ref.shape[0]
    buf_ref[0:CONV_HALO, :] = jnp.where(i > 0, uprev_ref[...], 0.0)
    buf_ref[CONV_HALO:, :] = u_ref[...]
    for r0 in range(0, tm, CONV_ROWS):
        acc = _conv_rows(buf_ref, sh_ref, r0, cw_ref, cb_ref)
        c_ref[r0:r0 + CONV_ROWS, :] = _ln_silu(acc, cg_ref[...], cbeta_ref[...]).astype(c_ref.dtype)
    _post_body(h_ref[...], o_ref[...], c_ref[...], p_ref, *rest)


def _post(h, o, c_or_u, p, conv_params, consts, seq=None):
    t, d = h.shape
    tm = min(ROW_TILE, t)
    row = lambda w: pl.BlockSpec((tm, w), lambda i: (i, 0))
    ch = c_or_u.shape[1]
    common = dict(
        out_shape=jax.ShapeDtypeStruct((t, d), F32),
        grid=(t // tm,),
        out_specs=row(d),
        compiler_params=pltpu.CompilerParams(
            dimension_semantics=("arbitrary",), vmem_limit_bytes=VMEM_LIMIT_BYTES),
    )
    const_specs = [_const_spec(a.shape) for a in consts]
    acc = pltpu.VMEM((tm, d), F32)
    if conv_params is None:
        return pl.pallas_call(
            _post_kernel,
            in_specs=[row(d), row(o.shape[1]), row(ch), row(p.shape[1])] + const_specs,
            scratch_shapes=[acc], name="post_mix", **common,
        )(h, o, c_or_u, p, *consts)
    assert seq % tm == 0 and tm % CONV_ROWS == 0
    hb = tm // CONV_HALO
    return pl.pallas_call(
        functools.partial(_post_conv_kernel, tiles_per_seq=seq // tm),
        in_specs=[row(d), row(o.shape[1]),
                  pl.BlockSpec((CONV_HALO, ch), lambda i: (jnp.maximum(i * hb - 1, 0), 0)),
                  row(ch), row(p.shape[1])] + [_const_spec(a.shape) for a in conv_params] + const_specs,
        scratch_shapes=[acc, pltpu.VMEM((tm + CONV_HALO, ch), F32),
                        pltpu.VMEM((SUBLANES - 1, _conv_span() - SUBLANES, ch), F32),
                        pltpu.VMEM((tm, ch), BF16)],
        name="post_conv_mix", **common,
    )(h, o, c_or_u, c_or_u, p, *conv_params, *consts)


def kernel(x_prompt, x_sample, cache_k, cache_v, state_conv, page_table, p_prompt, p_sample, rel_bias, ffn1_norm, ffn1_w_gate, ffn1_w_up, ffn1_w_down, mix_norm, w_in, lambda_q1, lambda_k1, lambda_q2, lambda_k2, attn_subln, conv_w, conv_b, conv_ln_g, conv_ln_b, w_out, ffn2_norm, ffn2_w_gate, ffn2_w_up, ffn2_w_down, ple_norm, w_ple_gate, w_ple_proj, final_norm):
    b, seq, d = x_prompt.shape
    db, n_new, _ = x_sample.shape
    assert cache_k.shape[0] == 1, "one layer"
    page = cache_k.shape[2]
    tile = min(ATT_TK, seq)
    bf = lambda w: w[0].astype(BF16)

    bias_p, bias_d, lam = _bias_tables(rel_bias, lambda_q1, lambda_k1, lambda_q2, lambda_k2,
                                       min(ATT_TQ, seq), tile, n_new, page)
    ffn1 = (ffn1_norm, bf(ffn1_w_gate), bf(ffn1_w_up), bf(ffn1_w_down), mix_norm, bf(w_in))
    post_consts = (bf(w_out), ffn2_norm, bf(ffn2_w_gate), bf(ffn2_w_up), bf(ffn2_w_down), ple_norm,
                   bf(w_ple_gate), bf(w_ple_proj), final_norm.reshape(1, d))
    conv_params = (conv_w[0], conv_b, conv_ln_g, conv_ln_b)
    kv5 = lambda a, n, s: a.reshape(1, n, s, N_HEADS, HEAD_W)

    hs, qs, ks, vs, _, _, us = _ffn_inproj(x_sample.reshape(-1, d), *ffn1, tile)
    os_ = _decode_attention(page_table, cache_k[0], cache_v[0], qs.astype(F32).reshape(db, n_new, ATTN_W),
                            ks, vs, bias_d, lam, attn_subln)
    cs, conv_s = _sample_conv(state_conv[0], us.reshape(db, n_new, -1), *conv_params)
    y_s = _post(hs, os_.reshape(db * n_new, ATTN_W), cs.reshape(db * n_new, -1),
                p_sample[0].reshape(db * n_new, -1), None, post_consts)

    hp, qp, kp, vp, kbp, vtp, up = _ffn_inproj(x_prompt.reshape(-1, d), *ffn1, tile)
    op = _prompt_attention(qp, kbp, vtp, bias_p, lam, attn_subln.reshape(HEAD_W, 1), b, seq)
    y_p = _post(hp, op, up, p_prompt[0].reshape(b * seq, -1), conv_params, post_consts, seq)

    hist = CONV_W - 1
    conv_p = up.reshape(b, seq, -1)[:, seq - hist:]
    return (y_p.reshape(b, seq, d), y_s.reshape(db, n_new, d),
            kv5(kp, b, seq), kv5(vp, b, seq), conv_p[None],
            kv5(ks, db, n_new), kv5(vs, db, n_new), conv_s[None])
```

```python
import functools
import math

import jax
import jax.numpy as jnp
from jax import lax
from jax.experimental import pallas as pl
from jax.experimental.pallas import tpu as pltpu

N_HEADS = 4
HEAD_D = 64
HEAD_W = 2 * HEAD_D
ATTN_W = N_HEADS * HEAD_W
CONV_W = 31
N_BUCKETS = 32
MAX_EXACT = N_BUCKETS // 2
MAX_DIST = 128
EPS = 1e-6
NEG = -1e30
LAM_INIT = 0.8 - 0.6 * math.exp(-0.3 * 0)
LOG2E = math.log2(math.e)

LANES = 128
SUBLANES = 8
VMEM_LIMIT_BYTES = 56 * 1024 * 1024

FF_CHUNK = 256
ROW_TILE = 512
ATT_TQ = 512
ATT_TK = 256
DENOM_ROWS = 16
DEC_PAGES_PER_STEP = 16
CONV_HALO = 32
CONV_ROWS = 128
CONV_JOB_ROWS = 32
BF16 = jnp.bfloat16
F32 = jnp.float32


def _rms(x, g):
    return x * lax.rsqrt(jnp.mean(x * x, axis=-1, keepdims=True) + EPS) * g


def _const_spec(shape):
    nd = len(shape)
    return pl.BlockSpec(shape, lambda *_: (0,) * nd, pipeline_mode=pl.Buffered(1))


def _head_cols(h):
    return slice(h * HEAD_W, (h + 1) * HEAD_W)


def _zero_from(x):
    bits = pltpu.bitcast(x, jnp.uint32)
    bits = lax.shift_right_logical(lax.shift_right_logical(bits, jnp.uint32(16)), jnp.uint32(16))
    return bits.astype(F32)


def _swiglu_half(x, g_ref, wg_ref, wu_ref, wd_ref, acc_ref, side_jobs=()):
    u = _rms(x, g_ref[...]).astype(BF16)
    n_chunks = wg_ref.shape[1] // FF_CHUNK
    jobs = list(side_jobs)
    assert len(jobs) <= 2 * n_chunks
    tile = (slice(0, SUBLANES), slice(0, LANES))
    pending = []

    def run_job(after):
        if jobs:
            out = jobs.pop(0)(_zero_from(after[tile] + after[-SUBLANES:, -LANES:]))
            if out is not None:
                started.append(out)

    for c in range(n_chunks):
        cols = slice(c * FF_CHUNK, (c + 1) * FF_CHUNK)
        started = []
        g = jnp.dot(u, wg_ref[:, cols], preferred_element_type=F32)
        up = jnp.dot(u, wu_ref[:, cols], preferred_element_type=F32)
        act = g * jax.nn.sigmoid(g) * up
        a = act.astype(BF16)
        run_job(act)
        d = jnp.dot(a, wd_ref[cols, :], preferred_element_type=F32)
        if c == 0:
            acc_ref[...] = d
        else:
            acc_ref[...] += d
        for z in pending:
            acc_ref[tile] += _zero_from(z)
        run_job(d)
        pending = started
    for z in pending:
        acc_ref[tile] += _zero_from(z)
    return x + 0.5 * acc_ref[...]


def _ffn_inproj_kernel(x_ref, g1_ref, wg_ref, wu_ref, wd_ref, gm_ref, win_ref,
                       h_ref, q_ref, k_ref, v_ref, kb_ref, vt_ref, u_ref, acc_ref):
    h = _swiglu_half(x_ref[...], g1_ref, wg_ref, wu_ref, wd_ref, acc_ref)
    h_ref[...] = h
    un = _rms(h, gm_ref[...]).astype(BF16)
    a = ATTN_W
    tm = x_ref.shape[0]
    q = jnp.dot(un, win_ref[:, 0:a], preferred_element_type=F32)
    q_ref[...] = (q * (HEAD_D ** -0.5 * LOG2E)).astype(BF16)
    k = jnp.dot(un, win_ref[:, a:2 * a], preferred_element_type=F32)
    kb_ref[...] = k.astype(BF16)
    v = jnp.dot(un, win_ref[:, 2 * a:3 * a], preferred_element_type=F32)
    for hd in range(N_HEADS):
        k_ref[pl.ds(hd, tm, stride=N_HEADS), :] = k[:, _head_cols(hd)]
        v_ref[pl.ds(hd, tm, stride=N_HEADS), :] = v[:, _head_cols(hd)]
    tk = vt_ref.shape[2]
    for s in range(vt_ref.shape[0]):
        vt_ref[s] = v[s * tk:(s + 1) * tk, :].T.astype(BF16)
    ga = jnp.dot(un, win_ref[:, 3 * a:4 * a], preferred_element_type=F32)
    gb = jnp.dot(un, win_ref[:, 4 * a:5 * a], preferred_element_type=F32)
    u_ref[...] = ga * jax.nn.sigmoid(gb)


def _ffn_inproj(x, g1, wg, wu, wd, gm, win, tk):
    t, d = x.shape
    tm = min(ROW_TILE, t)
    tk = min(tk, tm)
    assert t % tm == 0 and tm % tk == 0
    row = lambda w: pl.BlockSpec((tm, w), lambda i: (i, 0))
    outs = [
        jax.ShapeDtypeStruct((t, d), F32),
        jax.ShapeDtypeStruct((t, ATTN_W), BF16),
        jax.ShapeDtypeStruct((t * N_HEADS, HEAD_W), F32),
        jax.ShapeDtypeStruct((t * N_HEADS, HEAD_W), F32),
        jax.ShapeDtypeStruct((t, ATTN_W), BF16),
        jax.ShapeDtypeStruct((t // tk, ATTN_W, tk), BF16),
        jax.ShapeDtypeStruct((t, ATTN_W), F32),
    ]
    consts = [g1, wg, wu, wd, gm, win]
    return pl.pallas_call(
        _ffn_inproj_kernel,
        out_shape=outs,
        grid=(t // tm,),
        in_specs=[row(d)] + [_const_spec(c.shape) for c in consts],
        out_specs=[row(d), row(ATTN_W),
                   pl.BlockSpec((tm * N_HEADS, HEAD_W), lambda i: (i, 0)),
                   pl.BlockSpec((tm * N_HEADS, HEAD_W), lambda i: (i, 0)),
                   row(ATTN_W),
                   pl.BlockSpec((tm // tk, ATTN_W, tk), lambda i: (i, 0, 0)),
                   row(ATTN_W)],
        scratch_shapes=[pltpu.VMEM((tm, d), F32)],
        compiler_params=pltpu.CompilerParams(
            dimension_semantics=("arbitrary",), vmem_limit_bytes=VMEM_LIMIT_BYTES),
        name="ffn1_inproj",
    )(x, *consts)


def _bias_of_distance(dist, rb_ref, h):
    n = jnp.maximum(dist, 0)
    nf = jnp.maximum(n, MAX_EXACT).astype(F32)
    large = MAX_EXACT + (jnp.log(nf / MAX_EXACT) / math.log(MAX_DIST / MAX_EXACT)
                         * (N_BUCKETS - MAX_EXACT)).astype(jnp.int32)
    large = jnp.minimum(large, N_BUCKETS - 1)
    bucket = jnp.where(n < MAX_EXACT, n, large)
    far = rb_ref[N_BUCKETS - 1, h]
    val = jnp.zeros(dist.shape, F32)
    for i in range(N_BUCKETS - 1):
        val = jnp.where(bucket == i, rb_ref[i, h] - far, val)
    return jnp.where(dist >= 0, val * LOG2E, NEG)


def _bias_kernel(rb_ref, lq1_ref, lk1_ref, lq2_ref, lk2_ref, bp_ref, bd_ref, lam_ref):
    tk, tq = bp_ref.shape[2], bp_ref.shape[3]
    c = lax.broadcasted_iota(jnp.int32, (tk, tq), 0)
    r = lax.broadcasted_iota(jnp.int32, (tk, tq), 1)
    n_new = bd_ref.shape[1] // 2
    page = bd_ref.shape[2] // 2
    qi = lax.broadcasted_iota(jnp.int32, (2 * n_new, page), 0) % n_new
    cj = lax.broadcasted_iota(jnp.int32, (2 * n_new, page), 1)
    for h in range(N_HEADS):
        for n in range(bp_ref.shape[1]):
            bp_ref[h, n] = _bias_of_distance(r - c - (n - 1) * tk, rb_ref, h)
        bd_ref[h, :, 0:page] = _bias_of_distance(qi + page - cj, rb_ref, h)
        new = _bias_of_distance(qi - cj, rb_ref, h)
        bd_ref[h, :, page:2 * page] = jnp.where(cj < n_new, new, NEG)
    e1 = jnp.exp(jnp.sum(lq1_ref[...] * lk1_ref[...], axis=-1, keepdims=True))
    e2 = jnp.exp(jnp.sum(lq2_ref[...] * lk2_ref[...], axis=-1, keepdims=True))
    lam_ref[...] = jnp.broadcast_to(e1 - e2 + LAM_INIT, lam_ref.shape)


def _bias_tables(rel_bias, lq1, lk1, lq2, lk2, tq, tk, n_new, page):
    vm = lambda shape: pl.BlockSpec(shape, lambda: (0,) * len(shape))
    assert tq % tk == 0 and tk >= MAX_DIST
    shapes = [(N_HEADS, tq // tk + 1, tk, tq), (N_HEADS, 2 * n_new, 2 * page), (SUBLANES, max(LANES, tq))]
    return pl.pallas_call(
        _bias_kernel,
        out_shape=[jax.ShapeDtypeStruct(s, F32) for s in shapes],
        in_specs=[pl.BlockSpec(memory_space=pltpu.SMEM)] + [vm(lq1.shape)] * 4,
        out_specs=[vm(s) for s in shapes],
        name="bias_tables",
    )(rel_bias, lq1, lk1, lq2, lk2)


def _stack_maps(qh):
    lane = lax.broadcasted_iota(jnp.int32, qh.shape, 1)
    zero = jnp.zeros_like(qh)
    return jnp.concatenate([jnp.where(lane < HEAD_D, qh, zero),
                            jnp.where(lane < HEAD_D, zero, qh)], axis=0)


def _prompt_attn_kernel(q_ref, k_ref, vt_ref, bias_ref, lam_ref, gt_ref, o_ref, qs_ref, m_ref, acc_ref, s_ref):
    tq = q_ref.shape[0]
    tk = bias_ref.shape[2]
    first_key_tile = pl.program_id(1) * (tq // tk)
    last_key_tile = first_key_tile + tq // tk - 1
    for h in range(N_HEADS):
        qs_ref[h] = _stack_maps(q_ref[:, _head_cols(h)])
    m_ref[...] = jnp.full(m_ref.shape, NEG, F32)
    acc_ref[...] = jnp.zeros(acc_ref.shape, F32)

    ones = jnp.ones((DENOM_ROWS, tk), BF16)

    def scores(j, h):
        start = pl.multiple_of(j * tk, tk)
        kt = k_ref[pl.ds(start, tk), _head_cols(h)]
        return lax.dot_general(kt, qs_ref[h], (((1,), (1,)), ((), ())), preferred_element_type=F32)

    def update(j, h, s, near):
        if near:
            b = bias_ref[h, j - first_key_tile + 1]
            s = s + jnp.concatenate([b, b], axis=1)
        m_old = m_ref[h]
        m_new = jnp.maximum(m_old, jnp.max(s, axis=0, keepdims=True))
        p = jnp.exp2(s - m_new).astype(BF16)
        alpha = jnp.exp2(m_old - m_new)
        m_ref[h] = m_new
        vt = jnp.concatenate([vt_ref[j, _head_cols(h), :], ones], axis=0)
        acc_ref[h] = alpha * acc_ref[h] + jnp.dot(vt, p, preferred_element_type=F32)

    def step(j, near):
        nxt = jnp.minimum(j + 1, last_key_tile)
        for h in range(N_HEADS):
            s = s_ref[h]
            s_ref[h] = scores(nxt, h)
            update(j, h, s, near)

    def far_body(j, carry):
        step(j, False)
        return carry

    def near_body(j, carry):
        step(j, True)
        return carry

    first_near = jnp.maximum(first_key_tile - 1, 0)
    for h in range(N_HEADS):
        s_ref[h] = scores(0, h)
    lax.fori_loop(0, first_near, far_body, 0)
    lax.fori_loop(first_near, last_key_tile + 1, near_body, 0)

    lam = lam_ref[0:1, 0:tq]
    for h in range(N_HEADS):
        acc = acc_ref[h, 0:HEAD_W, :]
        l = acc_ref[h, HEAD_W:HEAD_W + 1, :]
        o = acc[:, :tq] / l[:, :tq] - lam * (acc[:, tq:] / l[:, tq:])
        o = o * lax.rsqrt(jnp.mean(o * o, axis=0, keepdims=True) + EPS) * gt_ref[...] * (1.0 - LAM_INIT)
        o_ref[:, _head_cols(h)] = o.T.astype(o_ref.dtype)


def _prompt_attention(q, kb, vt, bias_p, lam, gt, batch, seq):
    tk, tq = bias_p.shape[2], bias_p.shape[3]
    nq = seq // tq
    nk = seq // tk
    return pl.pallas_call(
        _prompt_attn_kernel,
        out_shape=jax.ShapeDtypeStruct((batch * seq, ATTN_W), BF16),
        grid=(batch, nq),
        in_specs=[pl.BlockSpec((tq, ATTN_W), lambda b, i: (b * nq + i, 0)),
                  pl.BlockSpec((seq, ATTN_W), lambda b, i: (b, 0)),
                  pl.BlockSpec((nk, ATTN_W, tk), lambda b, i: (b, 0, 0)),
                  _const_spec(bias_p.shape), _const_spec(lam.shape), _const_spec(gt.shape)],
        out_specs=pl.BlockSpec((tq, ATTN_W), lambda b, i: (b * nq + i, 0)),
        scratch_shapes=[pltpu.VMEM((N_HEADS, 2 * tq, HEAD_W), BF16),
                        pltpu.VMEM((N_HEADS, 1, 2 * tq), F32),
                        pltpu.VMEM((N_HEADS, HEAD_W + DENOM_ROWS, 2 * tq), F32),
                        pltpu.VMEM((N_HEADS, tk, 2 * tq), F32)],
        compiler_params=pltpu.CompilerParams(
            dimension_semantics=("arbitrary", "arbitrary"), vmem_limit_bytes=VMEM_LIMIT_BYTES),
        name="prompt_attention",
    )(q, kb, vt, bias_p, lam, gt)


def _softmax_step(s, m, l, acc, v):
    m_new = jnp.maximum(m, jnp.max(s, axis=-1, keepdims=True))
    p = jnp.exp2(s - m_new)
    alpha = jnp.exp2(m - m_new)
    l = alpha * l + jnp.sum(p, axis=-1, keepdims=True)
    acc = alpha * acc + jnp.dot(p.astype(BF16), v, preferred_element_type=F32)
    return m_new, l, acc


def _decode_attn_kernel(pt_ref, ck_hbm, cv_hbm, q_ref, kn_ref, vn_ref, bias_ref, lam_ref, g_ref, o_ref,
                        kbuf, vbuf, sem, *, chunk_pages, page):
    b = pl.program_id(0)
    nb = pl.num_programs(0)
    n_chunks = kbuf.shape[0]
    rows = page * N_HEADS
    n_new = q_ref.shape[1]

    def chunk_copies(seq, c):
        copies = []
        for j in range(chunk_pages):
            pg = pt_ref[(seq * n_chunks + c) * chunk_pages + j]
            src = pl.ds(pl.multiple_of(pg * rows, rows), rows)
            dst = pl.ds(j * rows, rows)
            copies.append(pltpu.make_async_copy(ck_hbm.at[src], kbuf.at[c, dst], sem.at[0, c]))
            copies.append(pltpu.make_async_copy(cv_hbm.at[src], vbuf.at[c, dst], sem.at[1, c]))
        return copies

    @pl.when(b == 0)
    def _():
        for c in range(n_chunks):
            for cp in chunk_copies(0, c):
                cp.start()

    nxt = jnp.minimum(b + 1, nb - 1)
    qs = [_stack_maps(q_ref[0, :, _head_cols(h)]).astype(BF16) for h in range(N_HEADS)]
    state = [(jnp.full((2 * n_new, 1), NEG, F32), jnp.zeros((2 * n_new, 1), F32),
              jnp.zeros((2 * n_new, HEAD_W), F32)) for _ in range(N_HEADS)]
    for c in range(n_chunks):
        for cp in chunk_copies(b, c):
            cp.wait()
        for h in range(N_HEADS):
            kt = jnp.concatenate(
                [kbuf.at[c][pl.ds(j * rows + h, page, stride=N_HEADS), :].astype(BF16)
                 for j in range(chunk_pages)], axis=0)
            vt = jnp.concatenate(
                [vbuf.at[c][pl.ds(j * rows + h, page, stride=N_HEADS), :].astype(BF16)
                 for j in range(chunk_pages)], axis=0)
            s = lax.dot_general(qs[h], kt, (((1,), (1,)), ((), ())), preferred_element_type=F32)
            if c == n_chunks - 1:
                near = (chunk_pages - 1) * page
                s = jnp.concatenate([s[:, :near], s[:, near:] + bias_ref[h, :, 0:page]], axis=1)
            state[h] = _softmax_step(s, *state[h], vt)
        for cp in chunk_copies(nxt, c):
            cp.start()

    lam = lam_ref[0:1, 0:HEAD_W]
    pad = jnp.zeros((page - n_new, HEAD_W), BF16)
    for h in range(N_HEADS):
        kn = kn_ref[pl.ds(h, n_new, stride=N_HEADS), :]
        vn = vn_ref[pl.ds(h, n_new, stride=N_HEADS), :]
        kt = jnp.concatenate([kn.astype(BF16), pad], axis=0)
        vt = jnp.concatenate([vn.astype(BF16), pad], axis=0)
        s = lax.dot_general(qs[h], kt, (((1,), (1,)), ((), ())), preferred_element_type=F32)
        s = s + bias_ref[h, :, page:2 * page]
        _, l, acc = _softmax_step(s, *state[h], vt)
        o = acc[:n_new] / l[:n_new] - lam * (acc[n_new:] / l[n_new:])
        o_ref[0, :, _head_cols(h)] = (_rms(o, g_ref[...]) * (1.0 - LAM_INIT)).astype(o_ref.dtype)

    @pl.when(b == nb - 1)
    def _():
        for c in range(n_chunks):
            for cp in chunk_copies(nxt, c):
                cp.wait()


def _decode_attention(page_table, cache_k, cache_v, q3, k_rows, v_rows, bias_d, lam, g):
    db, n_pages = page_table.shape
    n_pool, page = cache_k.shape[0], cache_k.shape[1]
    n_new = q3.shape[1]
    pps = min(DEC_PAGES_PER_STEP, n_pages)
    assert n_pages % pps == 0
    n_chunks = n_pages // pps
    rows = page * N_HEADS
    ck = cache_k.reshape(n_pool * rows, HEAD_W)
    cv = cache_v.reshape(n_pool * rows, HEAD_W)
    pt = page_table.reshape(-1)

    tok = pl.BlockSpec((1, n_new, ATTN_W), lambda b, pt: (b, 0, 0))
    new_rows = pl.BlockSpec((n_new * N_HEADS, HEAD_W), lambda b, pt: (b, 0))
    const = lambda a: pl.BlockSpec(a.shape, lambda b, pt: (0,) * a.ndim)
    hbm = pl.BlockSpec(memory_space=pl.ANY)
    grid_spec = pltpu.PrefetchScalarGridSpec(
        num_scalar_prefetch=1,
        grid=(db,),
        in_specs=[hbm, hbm, tok, new_rows, new_rows, const(bias_d), const(lam), const(g)],
        out_specs=tok,
        scratch_shapes=[pltpu.VMEM((n_chunks, pps * rows, HEAD_W), F32),
                        pltpu.VMEM((n_chunks, pps * rows, HEAD_W), F32),
                        pltpu.SemaphoreType.DMA((2, n_chunks))],
    )
    return pl.pallas_call(
        functools.partial(_decode_attn_kernel, chunk_pages=pps, page=page),
        out_shape=jax.ShapeDtypeStruct((db, n_new, ATTN_W), BF16),
        grid_spec=grid_spec,
        compiler_params=pltpu.CompilerParams(
            dimension_semantics=("arbitrary",), vmem_limit_bytes=VMEM_LIMIT_BYTES),
        name="decode_attention",
    )(pt, ck, cv, q3, k_rows, v_rows, bias_d, lam, g)


def _ln_silu(c, g, b):
    mu = jnp.mean(c, axis=-1, keepdims=True)
    xc = c - mu
    y = xc * lax.rsqrt(jnp.mean(xc * xc, axis=-1, keepdims=True) + EPS) * g + b
    return y * jax.nn.sigmoid(y)


def _conv_span():
    return CONV_ROWS + SUBLANES * (CONV_HALO // SUBLANES)


def _conv_shift_copies(buf_ref, sh_ref, r0):
    for s in range(1, SUBLANES):
        sh_ref[s - 1] = buf_ref[r0 + s:r0 + s + _conv_span() - SUBLANES, :]


def _conv_taps(buf_ref, sh_ref, r0, rr, n, w_ref, b_ref, zero=None):
    off = CONV_HALO - (CONV_W - 1)
    bias = b_ref[...]
    if zero is not None:
        bias = bias + jnp.concatenate([zero[0:1]] * (bias.shape[1] // LANES), axis=1)
    acc = jnp.broadcast_to(bias, (n, bias.shape[1]))
    for j in range(CONV_W):
        a, s = divmod(off + j, SUBLANES)
        lo = rr + a * SUBLANES
        x = buf_ref[r0 + lo:r0 + lo + n, :] if s == 0 else sh_ref[s - 1, lo:lo + n, :]
        acc = acc + w_ref[j:j + 1, :] * x
    return acc


def _sample_conv_kernel(st_ref, u_ref, w_ref, b_ref, g_ref, beta_ref, c_ref, ns_ref, buf_ref):
    hist = st_ref.shape[1]
    n_new = u_ref.shape[1]
    buf_ref[:, 0:hist, :] = st_ref[...]
    buf_ref[:, hist:hist + n_new, :] = u_ref[...]
    acc = jnp.broadcast_to(b_ref[...][None], u_ref.shape)
    for j in range(CONV_W):
        acc = acc + w_ref[j:j + 1, :][None] * buf_ref[:, j:j + n_new, :]
    c_ref[...] = _ln_silu(acc, g_ref[...][None], beta_ref[...][None]).astype(c_ref.dtype)
    ns_ref[...] = buf_ref[:, n_new:n_new + hist, :]


def _sample_conv(state, u3, w, b, g, beta):
    db, hist, ch = state.shape
    n_new = u3.shape[1]
    bb = min(16, db)
    blk = lambda r: pl.BlockSpec((bb, r, ch), lambda i: (i, 0, 0))
    return pl.pallas_call(
        _sample_conv_kernel,
        out_shape=[jax.ShapeDtypeStruct((db, n_new, ch), BF16),
                   jax.ShapeDtypeStruct((db, hist, ch), F32)],
        grid=(db // bb,),
        in_specs=[blk(hist), blk(n_new), _const_spec(w.shape), _const_spec(b.shape),
                  _const_spec(g.shape), _const_spec(beta.shape)],
        out_specs=[blk(n_new), blk(hist)],
        scratch_shapes=[pltpu.VMEM((bb, hist + n_new, ch), F32)],
        compiler_params=pltpu.CompilerParams(dimension_semantics=("arbitrary",)),
        name="sample_conv",
    )(state, u3, w, b, g, beta)


def _post_body(h, o, c, p_ref, wo_ref, g2_ref, wg_ref, wu_ref, wd_ref, gp_ref, wpg_ref, wpp_ref, gf_ref,
               y_ref, acc_ref, side_jobs=()):
    a = o.shape[1]
    mix = (jnp.dot(o, wo_ref[0:a, :], preferred_element_type=F32)
           + jnp.dot(c, wo_ref[a:, :], preferred_element_type=F32))
    h = _swiglu_half(h + mix, g2_ref, wg_ref, wu_ref, wd_ref, acc_ref, side_jobs)
    gate = jax.nn.sigmoid(jnp.dot(_rms(h, gp_ref[...]).astype(BF16), wpg_ref[...], preferred_element_type=F32))
    h = h + jnp.dot(p_ref[...].astype(BF16), wpp_ref[...], preferred_element_type=F32) * gate
    y_ref[...] = _rms(h, gf_ref[...])


def _post_kernel(h_ref, o_ref, c_ref, p_ref, *rest):
    _post_body(h_ref[...], o_ref[...], c_ref[...], p_ref, *rest)


def _post_conv_kernel(h_ref, o_ref, ufirst_ref, uhalo_ref, unext_ref, p_ref, cw_ref, cb_ref, cg_ref, cbeta_ref,
                      *rest, tiles_per_seq):
    *rest, buf_ref, sh_ref, c_ref = rest
    i = pl.program_id(0)
    tm = unext_ref.shape[0]

    def conv_jobs(u_tile_ref, halo_fn):
        def fill(zero=None):
            del zero
            buf_ref[0:CONV_HALO, :] = halo_fn()
            buf_ref[CONV_HALO:, :] = u_tile_ref[...]

        def copies(r0, zero=None):
            del zero
            _conv_shift_copies(buf_ref, sh_ref, r0)

        def taps(r0, rr, zero=None):
            acc = _conv_taps(buf_ref, sh_ref, r0, rr, CONV_JOB_ROWS, cw_ref, cb_ref, zero)
            y = _ln_silu(acc, cg_ref[...], cbeta_ref[...])
            c_ref[r0 + rr:r0 + rr + CONV_JOB_ROWS, :] = y.astype(c_ref.dtype)
            t = y[0:SUBLANES]
            for r in range(SUBLANES, CONV_JOB_ROWS, SUBLANES):
                t = t + y[r:r + SUBLANES]
            return sum(t[:, l:l + LANES] for l in range(0, t.shape[1], LANES))

        jobs = [fill]
        for r0 in range(0, tm, CONV_ROWS):
            jobs.append(functools.partial(copies, r0))
            jobs += [functools.partial(taps, r0, rr) for rr in range(0, CONV_ROWS, CONV_JOB_ROWS)]
        return jobs

    @pl.when(i == 0)
    def _():
        for job in conv_jobs(ufirst_ref, lambda: jnp.zeros((CONV_HALO, ufirst_ref.shape[1]), F32)):
            job()

    def next_halo():
        starts_seq = (jnp.minimum(i + 1, pl.num_programs(0) - 1) % tiles_per_seq) == 0
        return jnp.where(starts_seq, 0.0, uhalo_ref[...])

    c = c_ref[...]
    _post_body(h_ref[...], o_ref[...], c, p_ref, *rest, side_jobs=conv_jobs(unext_ref, next_halo))


def _post(h, o, c_or_u, p, conv_params, consts, seq=None):
    t, d = h.shape
    tm = min(ROW_TILE, t)
    row = lambda w: pl.BlockSpec((tm, w), lambda i: (i, 0))
    ch = c_or_u.shape[1]
    common = dict(
        out_shape=jax.ShapeDtypeStruct((t, d), F32),
        grid=(t // tm,),
        out_specs=row(d),
        compiler_params=pltpu.CompilerParams(
            dimension_semantics=("arbitrary",), vmem_limit_bytes=VMEM_LIMIT_BYTES),
    )
    const_specs = [_const_spec(a.shape) for a in consts]
    acc = pltpu.VMEM((tm, d), F32)
    if conv_params is None:
        return pl.pallas_call(
            _post_kernel,
            in_specs=[row(d), row(o.shape[1]), row(ch), row(p.shape[1])] + const_specs,
            scratch_shapes=[acc], name="post_mix", **common,
        )(h, o, c_or_u, p, *consts)
    assert seq % tm == 0 and tm % CONV_ROWS == 0 and CONV_ROWS % CONV_JOB_ROWS == 0
    n = t // tm
    hb = tm // CONV_HALO
    nxt = lambda i: jnp.minimum(i + 1, n - 1)
    return pl.pallas_call(
        functools.partial(_post_conv_kernel, tiles_per_seq=seq // tm),
        in_specs=[row(d), row(o.shape[1]),
                  _const_spec((tm, ch)),
                  pl.BlockSpec((CONV_HALO, ch), lambda i: (jnp.maximum(nxt(i) * hb - 1, 0), 0)),
                  pl.BlockSpec((tm, ch), lambda i: (nxt(i), 0)),
                  row(p.shape[1])] + [_const_spec(a.shape) for a in conv_params] + const_specs,
        scratch_shapes=[acc, pltpu.VMEM((tm + CONV_HALO, ch), F32),
                        pltpu.VMEM((SUBLANES - 1, _conv_span() - SUBLANES, ch), F32),
                        pltpu.VMEM((tm, ch), BF16)],
        name="post_conv_mix", **common,
    )(h, o, c_or_u, c_or_u, c_or_u, p, *conv_params, *consts)


def kernel(x_prompt, x_sample, cache_k, cache_v, state_conv, page_table, p_prompt, p_sample, rel_bias, ffn1_norm, ffn1_w_gate, ffn1_w_up, ffn1_w_down, mix_norm, w_in, lambda_q1, lambda_k1, lambda_q2, lambda_k2, attn_subln, conv_w, conv_b, conv_ln_g, conv_ln_b, w_out, ffn2_norm, ffn2_w_gate, ffn2_w_up, ffn2_w_down, ple_norm, w_ple_gate, w_ple_proj, final_norm):
    b, seq, d = x_prompt.shape
    db, n_new, _ = x_sample.shape
    assert cache_k.shape[0] == 1, "one layer"
    page = cache_k.shape[2]
    tile = min(ATT_TK, seq)
    bf = lambda w: w[0].astype(BF16)

    bias_p, bias_d, lam = _bias_tables(rel_bias, lambda_q1, lambda_k1, lambda_q2, lambda_k2,
                                       min(ATT_TQ, seq), tile, n_new, page)
    ffn1 = (ffn1_norm, bf(ffn1_w_gate), bf(ffn1_w_up), bf(ffn1_w_down), mix_norm, bf(w_in))
    post_consts = (bf(w_out), ffn2_norm, bf(ffn2_w_gate), bf(ffn2_w_up), bf(ffn2_w_down), ple_norm,
                   bf(w_ple_gate), bf(w_ple_proj), final_norm.reshape(1, d))
    conv_params = (conv_w[0], conv_b, conv_ln_g, conv_ln_b)
    kv5 = lambda a, n, s: a.reshape(1, n, s, N_HEADS, HEAD_W)

    hs, qs, ks, vs, _, _, us = _ffn_inproj(x_sample.reshape(-1, d), *ffn1, tile)
    os_ = _decode_attention(page_table, cache_k[0], cache_v[0], qs.astype(F32).reshape(db, n_new, ATTN_W),
                            ks, vs, bias_d, lam, attn_subln)
    cs, conv_s = _sample_conv(state_conv[0], us.reshape(db, n_new, -1), *conv_params)
    y_s = _post(hs, os_.reshape(db * n_new, ATTN_W), cs.reshape(db * n_new, -1),
                p_sample[0].reshape(db * n_new, -1), None, post_consts)

    hp, qp, kp, vp, kbp, vtp, up = _ffn_inproj(x_prompt.reshape(-1, d), *ffn1, tile)
    op = _prompt_attention(qp, kbp, vtp, bias_p, lam, attn_subln.reshape(HEAD_W, 1), b, seq)
    y_p = _post(hp, op, up, p_prompt[0].reshape(b * seq, -1), conv_params, post_consts, seq)

    hist = CONV_W - 1
    conv_p = up.reshape(b, seq, -1)[:, seq - hist:]
    return (y_p.reshape(b, seq, d), y_s.reshape(db, n_new, d),
            kv5(kp, b, seq), kv5(vp, b, seq), conv_p[None],
            kv5(ks, db, n_new), kv5(vs, db, n_new), conv_s[None])
```

```python
import functools
import math

import jax
import jax.numpy as jnp
from jax import lax
from jax.experimental import pallas as pl
from jax.experimental.pallas import tpu as pltpu

N_HEADS = 4
HEAD_D = 64
HEAD_W = 2 * HEAD_D
ATTN_W = N_HEADS * HEAD_W
CONV_W = 31
N_BUCKETS = 32
MAX_EXACT = N_BUCKETS // 2
MAX_DIST = 128
EPS = 1e-6
NEG = -1e30
LAM_INIT = 0.8 - 0.6 * math.exp(-0.3 * 0)
LOG2E = math.log2(math.e)

LANES = 128
SUBLANES = 8
VMEM_LIMIT_BYTES = 56 * 1024 * 1024

FF_CHUNK = 256
ROW_TILE = 512
ATT_TQ = 512
ATT_TK = 256
DENOM_ROWS = 16
DEC_PAGES_PER_STEP = 16
CONV_HALO = 32
CONV_ROWS = 128
CONV_JOB_ROWS = 32
BF16 = jnp.bfloat16
F32 = jnp.float32


def _rms(x, g):
    return x * lax.rsqrt(jnp.mean(x * x, axis=-1, keepdims=True) + EPS) * g


def _const_spec(shape):
    nd = len(shape)
    return pl.BlockSpec(shape, lambda *_: (0,) * nd, pipeline_mode=pl.Buffered(1))


def _head_cols(h):
    return slice(h * HEAD_W, (h + 1) * HEAD_W)


def _zero_from(x):
    bits = pltpu.bitcast(x, jnp.uint32)
    bits = lax.shift_right_logical(lax.shift_right_logical(bits, jnp.uint32(16)), jnp.uint32(16))
    return bits.astype(F32)


def _swiglu_half(x, g_ref, wg_ref, wu_ref, wd_ref, acc_ref, side_jobs=()):
    u = _rms(x, g_ref[...]).astype(BF16)
    n_chunks = wg_ref.shape[1] // FF_CHUNK
    jobs = list(side_jobs)
    assert len(jobs) <= 2 * n_chunks
    tile = (slice(0, SUBLANES), slice(0, LANES))
    pending = []

    def run_job(after):
        if jobs:
            out = jobs.pop(0)(_zero_from(after[tile] + after[-SUBLANES:, -LANES:]))
            if out is not None:
                started.append(out)

    for c in range(n_chunks):
        cols = slice(c * FF_CHUNK, (c + 1) * FF_CHUNK)
        started = []
        g = jnp.dot(u, wg_ref[:, cols], preferred_element_type=F32)
        up = jnp.dot(u, wu_ref[:, cols], preferred_element_type=F32)
        act = g * jax.nn.sigmoid(g) * up
        a = act.astype(BF16)
        run_job(act)
        d = jnp.dot(a, wd_ref[cols, :], preferred_element_type=F32)
        if c == 0:
            acc_ref[...] = d
        else:
            acc_ref[...] += d
        for z in pending:
            acc_ref[tile] += _zero_from(z)
        run_job(d)
        pending = started
    for z in pending:
        acc_ref[tile] += _zero_from(z)
    return x + 0.5 * acc_ref[...]


def _ffn_inproj_kernel(x_ref, g1_ref, wg_ref, wu_ref, wd_ref, gm_ref, win_ref,
                       h_ref, q_ref, k_ref, v_ref, kb_ref, vt_ref, u_ref, acc_ref):
    h = _swiglu_half(x_ref[...], g1_ref, wg_ref, wu_ref, wd_ref, acc_ref)
    h_ref[...] = h
    un = _rms(h, gm_ref[...]).astype(BF16)
    a = ATTN_W
    tm = x_ref.shape[0]
    q = jnp.dot(un, win_ref[:, 0:a], preferred_element_type=F32)
    q_ref[...] = (q * (HEAD_D ** -0.5 * LOG2E)).astype(BF16)
    k = jnp.dot(un, win_ref[:, a:2 * a], preferred_element_type=F32)
    kb_ref[...] = k.astype(BF16)
    v = jnp.dot(un, win_ref[:, 2 * a:3 * a], preferred_element_type=F32)
    for hd in range(N_HEADS):
        k_ref[pl.ds(hd, tm, stride=N_HEADS), :] = k[:, _head_cols(hd)]
        v_ref[pl.ds(hd, tm, stride=N_HEADS), :] = v[:, _head_cols(hd)]
    tk = vt_ref.shape[2]
    for s in range(vt_ref.shape[0]):
        vt_ref[s] = v[s * tk:(s + 1) * tk, :].T.astype(BF16)
    ga = jnp.dot(un, win_ref[:, 3 * a:4 * a], preferred_element_type=F32)
    gb = jnp.dot(un, win_ref[:, 4 * a:5 * a], preferred_element_type=F32)
    u_ref[...] = ga * jax.nn.sigmoid(gb)


def _ffn_inproj(x, g1, wg, wu, wd, gm, win, tk):
    t, d = x.shape
    tm = min(ROW_TILE, t)
    tk = min(tk, tm)
    assert t % tm == 0 and tm % tk == 0
    row = lambda w: pl.BlockSpec((tm, w), lambda i: (i, 0))
    outs = [
        jax.ShapeDtypeStruct((t, d), F32),
        jax.ShapeDtypeStruct((t, ATTN_W), BF16),
        jax.ShapeDtypeStruct((t * N_HEADS, HEAD_W), F32),
        jax.ShapeDtypeStruct((t * N_HEADS, HEAD_W), F32),
        jax.ShapeDtypeStruct((t, ATTN_W), BF16),
        jax.ShapeDtypeStruct((t // tk, ATTN_W, tk), BF16),
        jax.ShapeDtypeStruct((t, ATTN_W), F32),
    ]
    consts = [g1, wg, wu, wd, gm, win]
    return pl.pallas_call(
        _ffn_inproj_kernel,
        out_shape=outs,
        grid=(t // tm,),
        in_specs=[row(d)] + [_const_spec(c.shape) for c in consts],
        out_specs=[row(d), row(ATTN_W),
                   pl.BlockSpec((tm * N_HEADS, HEAD_W), lambda i: (i, 0)),
                   pl.BlockSpec((tm * N_HEADS, HEAD_W), lambda i: (i, 0)),
                   row(ATTN_W),
                   pl.BlockSpec((tm // tk, ATTN_W, tk), lambda i: (i, 0, 0)),
                   row(ATTN_W)],
        scratch_shapes=[pltpu.VMEM((tm, d), F32)],
        compiler_params=pltpu.CompilerParams(
            dimension_semantics=("arbitrary",), vmem_limit_bytes=VMEM_LIMIT_BYTES),
        name="ffn1_inproj",
    )(x, *consts)


def _bias_of_distance(dist, rb_ref, h):
    n = jnp.maximum(dist, 0)
    nf = jnp.maximum(n, MAX_EXACT).astype(F32)
    large = MAX_EXACT + (jnp.log(nf / MAX_EXACT) / math.log(MAX_DIST / MAX_EXACT)
                         * (N_BUCKETS - MAX_EXACT)).astype(jnp.int32)
    large = jnp.minimum(large, N_BUCKETS - 1)
    bucket = jnp.where(n < MAX_EXACT, n, large)
    far = rb_ref[N_BUCKETS - 1, h]
    val = jnp.zeros(dist.shape, F32)
    for i in range(N_BUCKETS - 1):
        val = jnp.where(bucket == i, rb_ref[i, h] - far, val)
    return jnp.where(dist >= 0, val * LOG2E, NEG)


def _bias_kernel(rb_ref, lq1_ref, lk1_ref, lq2_ref, lk2_ref, bp_ref, bd_ref, lam_ref):
    tk, tq = bp_ref.shape[2], bp_ref.shape[3]
    c = lax.broadcasted_iota(jnp.int32, (tk, tq), 0)
    r = lax.broadcasted_iota(jnp.int32, (tk, tq), 1)
    n_new = bd_ref.shape[1] // 2
    page = bd_ref.shape[2] // 2
    qi = lax.broadcasted_iota(jnp.int32, (2 * n_new, page), 0) % n_new
    cj = lax.broadcasted_iota(jnp.int32, (2 * n_new, page), 1)
    for h in range(N_HEADS):
        for n in range(bp_ref.shape[1]):
            bp_ref[h, n] = _bias_of_distance(r - c - (n - 1) * tk, rb_ref, h)
        bd_ref[h, :, 0:page] = _bias_of_distance(qi + page - cj, rb_ref, h)
        new = _bias_of_distance(qi - cj, rb_ref, h)
        bd_ref[h, :, page:2 * page] = jnp.where(cj < n_new, new, NEG)
    e1 = jnp.exp(jnp.sum(lq1_ref[...] * lk1_ref[...], axis=-1, keepdims=True))
    e2 = jnp.exp(jnp.sum(lq2_ref[...] * lk2_ref[...], axis=-1, keepdims=True))
    lam_ref[...] = jnp.broadcast_to(e1 - e2 + LAM_INIT, lam_ref.shape)


def _bias_tables(rel_bias, lq1, lk1, lq2, lk2, tq, tk, n_new, page):
    vm = lambda shape: pl.BlockSpec(shape, lambda: (0,) * len(shape))
    assert tq % tk == 0 and tk >= MAX_DIST
    shapes = [(N_HEADS, tq // tk + 1, tk, tq), (N_HEADS, 2 * n_new, 2 * page), (SUBLANES, max(LANES, tq))]
    return pl.pallas_call(
        _bias_kernel,
        out_shape=[jax.ShapeDtypeStruct(s, F32) for s in shapes],
        in_specs=[pl.BlockSpec(memory_space=pltpu.SMEM)] + [vm(lq1.shape)] * 4,
        out_specs=[vm(s) for s in shapes],
        name="bias_tables",
    )(rel_bias, lq1, lk1, lq2, lk2)


def _stack_maps(qh):
    lane = lax.broadcasted_iota(jnp.int32, qh.shape, 1)
    zero = jnp.zeros_like(qh)
    return jnp.concatenate([jnp.where(lane < HEAD_D, qh, zero),
                            jnp.where(lane < HEAD_D, zero, qh)], axis=0)


def _prompt_attn_kernel(q_ref, k_ref, vt_ref, bias_ref, lam_ref, gt_ref, o_ref, qs_ref, m_ref, acc_ref, s_ref):
    tq = q_ref.shape[0]
    tk = bias_ref.shape[2]
    first_key_tile = pl.program_id(1) * (tq // tk)
    last_key_tile = first_key_tile + tq // tk - 1
    for h in range(N_HEADS):
        qs_ref[h] = _stack_maps(q_ref[:, _head_cols(h)])
    m_ref[...] = jnp.full(m_ref.shape, NEG, F32)
    acc_ref[...] = jnp.zeros(acc_ref.shape, F32)

    ones = jnp.ones((DENOM_ROWS, tk), BF16)

    def scores(j, h):
        start = pl.multiple_of(j * tk, tk)
        kt = k_ref[pl.ds(start, tk), _head_cols(h)]
        return lax.dot_general(kt, qs_ref[h], (((1,), (1,)), ((), ())), preferred_element_type=F32)

    def update(j, h, s, near):
        if near:
            b = bias_ref[h, j - first_key_tile + 1]
            s = s + jnp.concatenate([b, b], axis=1)
        m_old = m_ref[h]
        m_new = jnp.maximum(m_old, jnp.max(s, axis=0, keepdims=True))
        p = jnp.exp2(s - m_new).astype(BF16)
        alpha = jnp.exp2(m_old - m_new)
        m_ref[h] = m_new
        vt = jnp.concatenate([vt_ref[j, _head_cols(h), :], ones], axis=0)
        acc_ref[h] = alpha * acc_ref[h] + jnp.dot(vt, p, preferred_element_type=F32)

    def step(j, near):
        nxt = jnp.minimum(j + 1, last_key_tile)
        for h in range(N_HEADS):
            s = s_ref[h]
            s_ref[h] = scores(nxt, h)
            update(j, h, s, near)

    def far_body(j, carry):
        step(j, False)
        return carry

    def near_body(j, carry):
        step(j, True)
        return carry

    first_near = jnp.maximum(first_key_tile - 1, 0)
    for h in range(N_HEADS):
        s_ref[h] = scores(0, h)
    lax.fori_loop(0, first_near, far_body, 0)
    lax.fori_loop(first_near, last_key_tile + 1, near_body, 0)

    lam = lam_ref[0:1, 0:tq]
    for h in range(N_HEADS):
        acc = acc_ref[h, 0:HEAD_W, :]
        l = acc_ref[h, HEAD_W:HEAD_W + 1, :]
        o = acc[:, :tq] / l[:, :tq] - lam * (acc[:, tq:] / l[:, tq:])
        o = o * lax.rsqrt(jnp.mean(o * o, axis=0, keepdims=True) + EPS) * gt_ref[...] * (1.0 - LAM_INIT)
        o_ref[:, _head_cols(h)] = o.T.astype(o_ref.dtype)


def _prompt_attention(q, kb, vt, bias_p, lam, gt, batch, seq):
    tk, tq = bias_p.shape[2], bias_p.shape[3]
    nq = seq // tq
    nk = seq // tk
    return pl.pallas_call(
        _prompt_attn_kernel,
        out_shape=jax.ShapeDtypeStruct((batch * seq, ATTN_W), BF16),
        grid=(batch, nq),
        in_specs=[pl.BlockSpec((tq, ATTN_W), lambda b, i: (b * nq + i, 0)),
                  pl.BlockSpec((seq, ATTN_W), lambda b, i: (b, 0)),
                  pl.BlockSpec((nk, ATTN_W, tk), lambda b, i: (b, 0, 0)),
                  _const_spec(bias_p.shape), _const_spec(lam.shape), _const_spec(gt.shape)],
        out_specs=pl.BlockSpec((tq, ATTN_W), lambda b, i: (b * nq + i, 0)),
        scratch_shapes=[pltpu.VMEM((N_HEADS, 2 * tq, HEAD_W), BF16),
                        pltpu.VMEM((N_HEADS, 1, 2 * tq), F32),
                        pltpu.VMEM((N_HEADS, HEAD_W + DENOM_ROWS, 2 * tq), F32),
                        pltpu.VMEM((N_HEADS, tk, 2 * tq), F32)],
        compiler_params=pltpu.CompilerParams(
            dimension_semantics=("arbitrary", "arbitrary"), vmem_limit_bytes=VMEM_LIMIT_BYTES),
        name="prompt_attention",
    )(q, kb, vt, bias_p, lam, gt)


def _softmax_step(s, m, l, acc, v):
    m_new = jnp.maximum(m, jnp.max(s, axis=-1, keepdims=True))
    p = jnp.exp2(s - m_new)
    alpha = jnp.exp2(m - m_new)
    l = alpha * l + jnp.sum(p, axis=-1, keepdims=True)
    acc = alpha * acc + jnp.dot(p.astype(BF16), v, preferred_element_type=F32)
    return m_new, l, acc


def _decode_attn_kernel(pt_ref, ck_hbm, cv_hbm, q_ref, kn_ref, vn_ref, bias_ref, lam_ref, g_ref, o_ref,
                        kbuf, vbuf, sem, *, chunk_pages, page):
    b = pl.program_id(0)
    nb = pl.num_programs(0)
    n_chunks = kbuf.shape[0]
    rows = page * N_HEADS
    n_new = q_ref.shape[1]

    def chunk_copies(seq, c):
        copies = []
        for j in range(chunk_pages):
            pg = pt_ref[(seq * n_chunks + c) * chunk_pages + j]
            src = pl.ds(pl.multiple_of(pg * rows, rows), rows)
            dst = pl.ds(j * rows, rows)
            copies.append(pltpu.make_async_copy(ck_hbm.at[src], kbuf.at[c, dst], sem.at[0, c]))
            copies.append(pltpu.make_async_copy(cv_hbm.at[src], vbuf.at[c, dst], sem.at[1, c]))
        return copies

    def start_all(copies):
        for n, cp in enumerate(copies):
            cp.start(priority=n % 2)

    @pl.when(b == 0)
    def _():
        for c in range(n_chunks):
            start_all(chunk_copies(0, c))

    nxt = jnp.minimum(b + 1, nb - 1)
    qs = [_stack_maps(q_ref[0, :, _head_cols(h)]).astype(BF16) for h in range(N_HEADS)]
    state = [(jnp.full((2 * n_new, 1), NEG, F32), jnp.zeros((2 * n_new, 1), F32),
              jnp.zeros((2 * n_new, HEAD_W), F32)) for _ in range(N_HEADS)]
    for c in range(n_chunks):
        for cp in chunk_copies(b, c):
            cp.wait()
        for h in range(N_HEADS):
            kt = jnp.concatenate(
                [kbuf.at[c][pl.ds(j * rows + h, page, stride=N_HEADS), :].astype(BF16)
                 for j in range(chunk_pages)], axis=0)
            vt = jnp.concatenate(
                [vbuf.at[c][pl.ds(j * rows + h, page, stride=N_HEADS), :].astype(BF16)
                 for j in range(chunk_pages)], axis=0)
            s = lax.dot_general(qs[h], kt, (((1,), (1,)), ((), ())), preferred_element_type=F32)
            if c == n_chunks - 1:
                near = (chunk_pages - 1) * page
                s = jnp.concatenate([s[:, :near], s[:, near:] + bias_ref[h, :, 0:page]], axis=1)
            state[h] = _softmax_step(s, *state[h], vt)
        start_all(chunk_copies(nxt, c))

    lam = lam_ref[0:1, 0:HEAD_W]
    pad = jnp.zeros((page - n_new, HEAD_W), BF16)
    for h in range(N_HEADS):
        kn = kn_ref[pl.ds(h, n_new, stride=N_HEADS), :]
        vn = vn_ref[pl.ds(h, n_new, stride=N_HEADS), :]
        kt = jnp.concatenate([kn.astype(BF16), pad], axis=0)
        vt = jnp.concatenate([vn.astype(BF16), pad], axis=0)
        s = lax.dot_general(qs[h], kt, (((1,), (1,)), ((), ())), preferred_element_type=F32)
        s = s + bias_ref[h, :, page:2 * page]
        _, l, acc = _softmax_step(s, *state[h], vt)
        o = acc[:n_new] / l[:n_new] - lam * (acc[n_new:] / l[n_new:])
        o_ref[0, :, _head_cols(h)] = (_rms(o, g_ref[...]) * (1.0 - LAM_INIT)).astype(o_ref.dtype)

    @pl.when(b == nb - 1)
    def _():
        for c in range(n_chunks):
            for cp in chunk_copies(nxt, c):
                cp.wait()


def _decode_attention(page_table, cache_k, cache_v, q3, k_rows, v_rows, bias_d, lam, g):
    db, n_pages = page_table.shape
    n_pool, page = cache_k.shape[0], cache_k.shape[1]
    n_new = q3.shape[1]
    pps = min(DEC_PAGES_PER_STEP, n_pages)
    assert n_pages % pps == 0
    n_chunks = n_pages // pps
    rows = page * N_HEADS
    ck = cache_k.reshape(n_pool * rows, HEAD_W)
    cv = cache_v.reshape(n_pool * rows, HEAD_W)
    pt = page_table.reshape(-1)

    tok = pl.BlockSpec((1, n_new, ATTN_W), lambda b, pt: (b, 0, 0))
    new_rows = pl.BlockSpec((n_new * N_HEADS, HEAD_W), lambda b, pt: (b, 0))
    const = lambda a: pl.BlockSpec(a.shape, lambda b, pt: (0,) * a.ndim)
    hbm = pl.BlockSpec(memory_space=pl.ANY)
    grid_spec = pltpu.PrefetchScalarGridSpec(
        num_scalar_prefetch=1,
        grid=(db,),
        in_specs=[hbm, hbm, tok, new_rows, new_rows, const(bias_d), const(lam), const(g)],
        out_specs=tok,
        scratch_shapes=[pltpu.VMEM((n_chunks, pps * rows, HEAD_W), F32),
                        pltpu.VMEM((n_chunks, pps * rows, HEAD_W), F32),
                        pltpu.SemaphoreType.DMA((2, n_chunks))],
    )
    return pl.pallas_call(
        functools.partial(_decode_attn_kernel, chunk_pages=pps, page=page),
        out_shape=jax.ShapeDtypeStruct((db, n_new, ATTN_W), BF16),
        grid_spec=grid_spec,
        compiler_params=pltpu.CompilerParams(
            dimension_semantics=("arbitrary",), vmem_limit_bytes=VMEM_LIMIT_BYTES),
        name="decode_attention",
    )(pt, ck, cv, q3, k_rows, v_rows, bias_d, lam, g)


def _ln_silu(c, g, b):
    mu = jnp.mean(c, axis=-1, keepdims=True)
    xc = c - mu
    y = xc * lax.rsqrt(jnp.mean(xc * xc, axis=-1, keepdims=True) + EPS) * g + b
    return y * jax.nn.sigmoid(y)


def _conv_span():
    return CONV_ROWS + SUBLANES * (CONV_HALO // SUBLANES)


def _conv_shift_copies(buf_ref, sh_ref, r0):
    for s in range(1, SUBLANES):
        sh_ref[s - 1] = buf_ref[r0 + s:r0 + s + _conv_span() - SUBLANES, :]


def _conv_taps(buf_ref, sh_ref, r0, rr, n, w_ref, b_ref, zero=None):
    off = CONV_HALO - (CONV_W - 1)
    bias = b_ref[...]
    if zero is not None:
        bias = bias + jnp.concatenate([zero[0:1]] * (bias.shape[1] // LANES), axis=1)
    acc = jnp.broadcast_to(bias, (n, bias.shape[1]))
    for j in range(CONV_W):
        a, s = divmod(off + j, SUBLANES)
        lo = rr + a * SUBLANES
        x = buf_ref[r0 + lo:r0 + lo + n, :] if s == 0 else sh_ref[s - 1, lo:lo + n, :]
        acc = acc + w_ref[j:j + 1, :] * x
    return acc


def _sample_conv_kernel(st_ref, u_ref, w_ref, b_ref, g_ref, beta_ref, c_ref, ns_ref, buf_ref):
    hist = st_ref.shape[1]
    n_new = u_ref.shape[1]
    buf_ref[:, 0:hist, :] = st_ref[...]
    buf_ref[:, hist:hist + n_new, :] = u_ref[...]
    acc = jnp.broadcast_to(b_ref[...][None], u_ref.shape)
    for j in range(CONV_W):
        acc = acc + w_ref[j:j + 1, :][None] * buf_ref[:, j:j + n_new, :]
    c_ref[...] = _ln_silu(acc, g_ref[...][None], beta_ref[...][None]).astype(c_ref.dtype)
    ns_ref[...] = buf_ref[:, n_new:n_new + hist, :]


def _sample_conv(state, u3, w, b, g, beta):
    db, hist, ch = state.shape
    n_new = u3.shape[1]
    bb = min(16, db)
    blk = lambda r: pl.BlockSpec((bb, r, ch), lambda i: (i, 0, 0))
    return pl.pallas_call(
        _sample_conv_kernel,
        out_shape=[jax.ShapeDtypeStruct((db, n_new, ch), BF16),
                   jax.ShapeDtypeStruct((db, hist, ch), F32)],
        grid=(db // bb,),
        in_specs=[blk(hist), blk(n_new), _const_spec(w.shape), _const_spec(b.shape),
                  _const_spec(g.shape), _const_spec(beta.shape)],
        out_specs=[blk(n_new), blk(hist)],
        scratch_shapes=[pltpu.VMEM((bb, hist + n_new, ch), F32)],
        compiler_params=pltpu.CompilerParams(dimension_semantics=("arbitrary",)),
        name="sample_conv",
    )(state, u3, w, b, g, beta)


def _post_body(h, o, c, p_ref, wo_ref, g2_ref, wg_ref, wu_ref, wd_ref, gp_ref, wpg_ref, wpp_ref, gf_ref,
               y_ref, acc_ref, side_jobs=()):
    a = o.shape[1]
    mix = (jnp.dot(o, wo_ref[0:a, :], preferred_element_type=F32)
           + jnp.dot(c, wo_ref[a:, :], preferred_element_type=F32))
    h = _swiglu_half(h + mix, g2_ref, wg_ref, wu_ref, wd_ref, acc_ref, side_jobs)
    gate = jax.nn.sigmoid(jnp.dot(_rms(h, gp_ref[...]).astype(BF16), wpg_ref[...], preferred_element_type=F32))
    h = h + jnp.dot(p_ref[...].astype(BF16), wpp_ref[...], preferred_element_type=F32) * gate
    y_ref[...] = _rms(h, gf_ref[...])


def _post_kernel(h_ref, o_ref, c_ref, p_ref, *rest):
    _post_body(h_ref[...], o_ref[...], c_ref[...], p_ref, *rest)


def _post_conv_kernel(h_ref, o_ref, ufirst_ref, uhalo_ref, unext_ref, p_ref, cw_ref, cb_ref, cg_ref, cbeta_ref,
                      *rest, tiles_per_seq):
    *rest, buf_ref, sh_ref, c_ref = rest
    i = pl.program_id(0)
    tm = unext_ref.shape[0]

    def conv_jobs(u_tile_ref, halo_fn):
        def fill(zero=None):
            del zero
            buf_ref[0:CONV_HALO, :] = halo_fn()
            buf_ref[CONV_HALO:, :] = u_tile_ref[...]

        def copies(r0, zero=None):
            del zero
            _conv_shift_copies(buf_ref, sh_ref, r0)

        def taps(r0, rr, zero=None):
            acc = _conv_taps(buf_ref, sh_ref, r0, rr, CONV_JOB_ROWS, cw_ref, cb_ref, zero)
            y = _ln_silu(acc, cg_ref[...], cbeta_ref[...])
            c_ref[r0 + rr:r0 + rr + CONV_JOB_ROWS, :] = y.astype(c_ref.dtype)
            t = y[0:SUBLANES]
            for r in range(SUBLANES, CONV_JOB_ROWS, SUBLANES):
                t = t + y[r:r + SUBLANES]
            return sum(t[:, l:l + LANES] for l in range(0, t.shape[1], LANES))

        jobs = [fill]
        for r0 in range(0, tm, CONV_ROWS):
            jobs.append(functools.partial(copies, r0))
            jobs += [functools.partial(taps, r0, rr) for rr in range(0, CONV_ROWS, CONV_JOB_ROWS)]
        return jobs

    @pl.when(i == 0)
    def _():
        for job in conv_jobs(ufirst_ref, lambda: jnp.zeros((CONV_HALO, ufirst_ref.shape[1]), F32)):
            job()

    def next_halo():
        starts_seq = (jnp.minimum(i + 1, pl.num_programs(0) - 1) % tiles_per_seq) == 0
        return jnp.where(starts_seq, 0.0, uhalo_ref[...])

    c = c_ref[...]
    _post_body(h_ref[...], o_ref[...], c, p_ref, *rest, side_jobs=conv_jobs(unext_ref, next_halo))


def _post(h, o, c_or_u, p, conv_params, consts, seq=None):
    t, d = h.shape
    tm = min(ROW_TILE, t)
    row = lambda w: pl.BlockSpec((tm, w), lambda i: (i, 0))
    ch = c_or_u.shape[1]
    common = dict(
        out_shape=jax.ShapeDtypeStruct((t, d), F32),
        grid=(t // tm,),
        out_specs=row(d),
        compiler_params=pltpu.CompilerParams(
            dimension_semantics=("arbitrary",), vmem_limit_bytes=VMEM_LIMIT_BYTES),
    )
    const_specs = [_const_spec(a.shape) for a in consts]
    acc = pltpu.VMEM((tm, d), F32)
    if conv_params is None:
        return pl.pallas_call(
            _post_kernel,
            in_specs=[row(d), row(o.shape[1]), row(ch), row(p.shape[1])] + const_specs,
            scratch_shapes=[acc], name="post_mix", **common,
        )(h, o, c_or_u, p, *consts)
    assert seq % tm == 0 and tm % CONV_ROWS == 0 and CONV_ROWS % CONV_JOB_ROWS == 0
    n = t // tm
    hb = tm // CONV_HALO
    nxt = lambda i: jnp.minimum(i + 1, n - 1)
    return pl.pallas_call(
        functools.partial(_post_conv_kernel, tiles_per_seq=seq // tm),
        in_specs=[row(d), row(o.shape[1]),
                  _const_spec((tm, ch)),
                  pl.BlockSpec((CONV_HALO, ch), lambda i: (jnp.maximum(nxt(i) * hb - 1, 0), 0)),
                  pl.BlockSpec((tm, ch), lambda i: (nxt(i), 0)),
                  row(p.shape[1])] + [_const_spec(a.shape) for a in conv_params] + const_specs,
        scratch_shapes=[acc, pltpu.VMEM((tm + CONV_HALO, ch), F32),
                        pltpu.VMEM((SUBLANES - 1, _conv_span() - SUBLANES, ch), F32),
                        pltpu.VMEM((tm, ch), BF16)],
        name="post_conv_mix", **common,
    )(h, o, c_or_u, c_or_u, c_or_u, p, *conv_params, *consts)


def kernel(x_prompt, x_sample, cache_k, cache_v, state_conv, page_table, p_prompt, p_sample, rel_bias, ffn1_norm, ffn1_w_gate, ffn1_w_up, ffn1_w_down, mix_norm, w_in, lambda_q1, lambda_k1, lambda_q2, lambda_k2, attn_subln, conv_w, conv_b, conv_ln_g, conv_ln_b, w_out, ffn2_norm, ffn2_w_gate, ffn2_w_up, ffn2_w_down, ple_norm, w_ple_gate, w_ple_proj, final_norm):
    b, seq, d = x_prompt.shape
    db, n_new, _ = x_sample.shape
    assert cache_k.shape[0] == 1, "one layer"
    page = cache_k.shape[2]
    tile = min(ATT_TK, seq)
    bf = lambda w: w[0].astype(BF16)

    bias_p, bias_d, lam = _bias_tables(rel_bias, lambda_q1, lambda_k1, lambda_q2, lambda_k2,
                                       min(ATT_TQ, seq), tile, n_new, page)
    ffn1 = (ffn1_norm, bf(ffn1_w_gate), bf(ffn1_w_up), bf(ffn1_w_down), mix_norm, bf(w_in))
    post_consts = (bf(w_out), ffn2_norm, bf(ffn2_w_gate), bf(ffn2_w_up), bf(ffn2_w_down), ple_norm,
                   bf(w_ple_gate), bf(w_ple_proj), final_norm.reshape(1, d))
    conv_params = (conv_w[0], conv_b, conv_ln_g, conv_ln_b)
    kv5 = lambda a, n, s: a.reshape(1, n, s, N_HEADS, HEAD_W)

    hs, qs, ks, vs, _, _, us = _ffn_inproj(x_sample.reshape(-1, d), *ffn1, tile)
    os_ = _decode_attention(page_table, cache_k[0], cache_v[0], qs.astype(F32).reshape(db, n_new, ATTN_W),
                            ks, vs, bias_d, lam, attn_subln)
    cs, conv_s = _sample_conv(state_conv[0], us.reshape(db, n_new, -1), *conv_params)
    y_s = _post(hs, os_.reshape(db * n_new, ATTN_W), cs.reshape(db * n_new, -1),
                p_sample[0].reshape(db * n_new, -1), None, post_consts)

    hp, qp, kp, vp, kbp, vtp, up = _ffn_inproj(x_prompt.reshape(-1, d), *ffn1, tile)
    op = _prompt_attention(qp, kbp, vtp, bias_p, lam, attn_subln.reshape(HEAD_W, 1), b, seq)
    y_p = _post(hp, op, up, p_prompt[0].reshape(b * seq, -1), conv_params, post_consts, seq)

    hist = CONV_W - 1
    conv_p = up.reshape(b, seq, -1)[:, seq - hist:]
    return (y_p.reshape(b, seq, d), y_s.reshape(db, n_new, d),
            kv5(kp, b, seq), kv5(vp, b, seq), conv_p[None],
            kv5(ks, db, n_new), kv5(vs, db, n_new), conv_s[None])
```
